```python
import math
import jax, jax.numpy as jnp
from jax import lax
import numpy as np

D_MODEL = 1024
BATCH = 8
SEQ = 2048
DEPTH = 2
DEC_BATCH = 128
DEC_SEQ = 8
PAST_LEN = 8192
PAGE_SIZE = 128

RET_HEADS = 4
RET_DK = 128
RET_DV = 256
GLA_HEADS = 4
GLA_DK = 128
GLA_DV = 256
GLA_GATE_RANK = 16
GLA_TAU = 16.0
CHUNK = 64
RET_QK_W = RET_HEADS * RET_DK
RET_V_W = RET_HEADS * RET_DV
GLA_QK_W = GLA_HEADS * GLA_DK
GLA_V_W = GLA_HEADS * GLA_DV
MIX_W = RET_V_W + GLA_V_W
IN_SPLITS = (RET_QK_W, RET_QK_W, RET_V_W, RET_V_W, GLA_QK_W, GLA_QK_W, GLA_V_W, GLA_V_W, GLA_GATE_RANK)
IN_W = sum(IN_SPLITS)
D_FF = 2816
MLA_HEADS = 16
QK_NOPE = 64
QK_ROPE = 32
V_HEAD = 64
Q_LORA = 384
KV_LORA = 256
MLA_SCALE = (QK_NOPE + QK_ROPE) ** -0.5
Q_BLOCK = 128
N_EXPERTS = 8
TOP_K = 2
D_FF_EXPERT = 3584
ROPE_THETA = 10000.0
EPS = 1e-6

kernel_name = 'hybrid_retention_gla_mla_moe_step'


def rms_norm(x, g):
    xf = x.astype(jnp.float32)
    y = xf * lax.rsqrt(jnp.mean(xf * xf, axis=-1, keepdims=True) + EPS)
    return (y * g.astype(jnp.float32)).astype(x.dtype)


def head_layer_norm(x, g):
    xf = x.astype(jnp.float32)
    xc = xf - jnp.mean(xf, axis=-1, keepdims=True)
    return xc * lax.rsqrt(jnp.mean(xc * xc, axis=-1, keepdims=True) + EPS) * g.astype(jnp.float32)


def rope(x, pos):
    half = x.shape[-1] // 2
    inv = ROPE_THETA ** (-jnp.arange(half, dtype=jnp.float32) / half)
    ang = pos.astype(jnp.float32)[:, None] * inv[None, :]
    cos = jnp.cos(ang)[:, None, :]
    sin = jnp.sin(ang)[:, None, :]
    xf = x.astype(jnp.float32)
    x1, x2 = xf[..., :half], xf[..., half:]
    return jnp.concatenate([x1 * cos - x2 * sin, x1 * sin + x2 * cos], axis=-1).astype(x.dtype)


def to_chunks(a, chunk):
    B, T, H, d = a.shape
    return a.astype(jnp.float32).reshape(B, T // chunk, chunk, H, d).transpose(1, 0, 3, 2, 4)


def from_chunks(a):
    n, B, H, C, d = a.shape
    return a.transpose(1, 0, 3, 2, 4).reshape(B, n * C, H, d)


def retention_chunked(q, k, v, s0, chunk):
    log_gamma = jnp.log1p(-jnp.exp2(-5.0 - jnp.arange(RET_HEADS, dtype=jnp.float32)))
    idx = jnp.arange(chunk, dtype=jnp.float32)
    rel = idx[:, None] - idx[None, :]
    decay_ij = jnp.where(rel >= 0, jnp.exp(log_gamma[:, None, None] * jnp.maximum(rel, 0.0)), 0.0)
    q_dec = jnp.exp(log_gamma[:, None] * (idx + 1.0))[..., None]
    k_dec = jnp.exp(log_gamma[:, None] * (chunk - 1.0 - idx))[..., None]
    c_dec = jnp.exp(log_gamma * chunk)[:, None, None]

    def step(s, blk):
        qc, kc, vc = blk
        a = jnp.einsum('bhid,bhjd->bhij', qc, kc) * decay_ij
        o = jnp.einsum('bhij,bhjv->bhiv', a, vc) + jnp.einsum('bhid,bhdv->bhiv', qc, s) * q_dec
        s = s * c_dec + jnp.einsum('bhjd,bhjv->bhdv', kc * k_dec, vc)
        return s, o

    s, o = lax.scan(step, s0.astype(jnp.float32), (to_chunks(q, chunk), to_chunks(k, chunk), to_chunks(v, chunk)))
    return from_chunks(o), s


def gla_chunked(q, k, v, log_a, s0, chunk):
    mask = jnp.tril(jnp.ones((chunk, chunk), dtype=bool))[:, :, None]

    def step(s, blk):
        qc, kc, vc, gc = blk
        b = jnp.cumsum(gc, axis=2)
        diff = b[:, :, :, None, :] - b[:, :, None, :, :]
        decay = jnp.where(mask, jnp.exp(jnp.minimum(diff, 0.0)), 0.0)
        a = jnp.einsum('bhid,bhjd,bhijd->bhij', qc, kc, decay)
        o = jnp.einsum('bhij,bhjv->bhiv', a, vc) + jnp.einsum('bhid,bhdv->bhiv', qc * jnp.exp(b), s)
        b_last = b[:, :, -1, :]
        s = s * jnp.exp(b_last)[..., None] + jnp.einsum('bhjd,bhjv->bhdv', kc * jnp.exp(b_last[:, :, None, :] - b), vc)
        return s, o

    blks = (to_chunks(q, chunk), to_chunks(k, chunk), to_chunks(v, chunk), to_chunks(log_a, chunk))
    s, o = lax.scan(step, s0.astype(jnp.float32), blks)
    return from_chunks(o), s


def parallel_retention_gla(h, pos, s_ret, s_gla, w_in, w_gla_gate, b_gla_gate, g_ret_head, g_gla_head, w_out):
    B, T, _ = h.shape
    offs = np.cumsum(IN_SPLITS)[:-1].tolist()
    rq, rk, rv, rg, gq, gk, gv, gr, ga = jnp.split(h @ w_in, offs, axis=-1)
    chunk = math.gcd(T, CHUNK)
    rq = rope(rq.reshape(B, T, RET_HEADS, RET_DK), pos)
    rk = rope(rk.reshape(B, T, RET_HEADS, RET_DK), pos) * (RET_DK ** -0.5)
    o_ret, s_ret = retention_chunked(rq, rk, rv.reshape(B, T, RET_HEADS, RET_DV), s_ret, chunk)
    o_ret = head_layer_norm(o_ret, g_ret_head.reshape(RET_HEADS, RET_DV)).astype(h.dtype)
    o_ret = o_ret.reshape(B, T, RET_V_W) * jax.nn.silu(rg)
    log_a = jax.nn.log_sigmoid((ga @ w_gla_gate + b_gla_gate).astype(jnp.float32)) / GLA_TAU
    log_a = log_a.reshape(B, T, GLA_HEADS, GLA_DK)
    gq = gq.reshape(B, T, GLA_HEADS, GLA_DK) * (GLA_DK ** -0.5)
    o_gla, s_gla = gla_chunked(gq, gk.reshape(B, T, GLA_HEADS, GLA_DK), gv.reshape(B, T, GLA_HEADS, GLA_DV), log_a, s_gla, chunk)
    o_gla = rms_norm(o_gla, g_gla_head.reshape(GLA_HEADS, GLA_DV)).astype(h.dtype)
    o_gla = o_gla.reshape(B, T, GLA_V_W) * jax.nn.silu(gr)
    y = jnp.concatenate([o_ret, o_gla], axis=-1) @ w_out
    return y, s_ret, s_gla


def swiglu(h, w_gate, w_up, w_down):
    return (jax.nn.silu(h @ w_gate) * (h @ w_up)) @ w_down


def latent_attention(q_lat, q_pe, ckv, kpe, q_pos):
    s = jnp.einsum('bqhc,bkc->bhqk', q_lat, ckv).astype(jnp.float32)
    s = (s + jnp.einsum('bqhr,bkr->bhqk', q_pe, kpe).astype(jnp.float32)) * MLA_SCALE
    k_pos = jnp.arange(ckv.shape[1])
    s = jnp.where(k_pos[None, :] <= q_pos[:, None], s, -jnp.inf)
    p = jax.nn.softmax(s, axis=-1)
    return jnp.einsum('bhqk,bkc->bqhc', p.astype(ckv.dtype), ckv)


def blocked_latent_attention(q_lat, q_pe, ckv, kpe, q_pos):
    B, T, H, C = q_lat.shape
    nb = T // Q_BLOCK
    qb = q_lat.reshape(B, nb, Q_BLOCK, H, C).swapaxes(0, 1)
    pb = q_pe.reshape(B, nb, Q_BLOCK, H, QK_ROPE).swapaxes(0, 1)
    posb = q_pos.reshape(nb, Q_BLOCK)
    o = lax.map(lambda blk: latent_attention(blk[0], blk[1], ckv, kpe, blk[2]), (qb, pb, posb))
    return o.swapaxes(0, 1).reshape(B, T, H, C)


def mla_mixer(h, pos, past_ckv, past_kpe, w_dq, g_q, w_uq, w_dkv, g_kv, w_uk, w_uv, w_o):
    B, T, _ = h.shape
    c_q = rms_norm(h @ w_dq, g_q)
    q = (c_q @ w_uq).reshape(B, T, MLA_HEADS, QK_NOPE + QK_ROPE)
    q_pe = rope(q[..., QK_NOPE:], pos)
    q_lat = jnp.einsum('bthn,chn->bthc', q[..., :QK_NOPE], w_uk)
    kv = h @ w_dkv
    ckv = rms_norm(kv[..., :KV_LORA], g_kv)
    kpe = rope(kv[..., None, KV_LORA:], pos)[:, :, 0, :]
    if past_ckv is None:
        o_lat = blocked_latent_attention(q_lat, q_pe, ckv, kpe, pos)
    else:
        keys_ckv = jnp.concatenate([past_ckv.astype(ckv.dtype), ckv], axis=1)
        keys_kpe = jnp.concatenate([past_kpe.astype(kpe.dtype), kpe], axis=1)
        o_lat = latent_attention(q_lat, q_pe, keys_ckv, keys_kpe, pos)
    o = jnp.einsum('bthc,chv->bthv', o_lat, w_uv).reshape(B, T, MLA_HEADS * V_HEAD)
    return o @ w_o, ckv, kpe


def moe_swiglu(h, w_router, w_exp_gate, w_exp_up, w_exp_down):
    logits = (h @ w_router).astype(jnp.float32)
    top_v, top_i = lax.top_k(logits, TOP_K)
    top_w = jax.nn.softmax(top_v, axis=-1)
    gate = jnp.sum(jax.nn.one_hot(top_i, N_EXPERTS, dtype=jnp.float32) * top_w[..., None], axis=-2)
    y = jnp.zeros_like(h)
    for e in range(N_EXPERTS):
        ye = swiglu(h, w_exp_gate[e], w_exp_up[e], w_exp_down[e])
        y = y + gate[..., e:e + 1].astype(h.dtype) * ye
    return y


def setup_inputs(seed: int = 0) -> dict:
    key = jax.random.key(seed)
    ks = iter(jax.random.split(key, 40))
    f32 = jnp.float32

    def nrm(shape, scale):
        return jax.random.normal(next(ks), shape, f32) * scale

    def gain(shape):
        return 1.0 + 0.05 * jax.random.normal(next(ks), shape, f32)

    n_pages = PAST_LEN // PAGE_SIZE
    n_pool = (5 * DEC_BATCH * n_pages + 3) // 4
    inp = {}
    inp['x_prompt'] = nrm((BATCH, SEQ, D_MODEL), 1.0)
    inp['x_sample'] = nrm((DEC_BATCH, DEC_SEQ, D_MODEL), 1.0)
    inp['state_ret'] = nrm((DEC_BATCH, RET_HEADS, RET_DK, RET_DV), 0.5)
    inp['state_gla'] = nrm((DEC_BATCH, GLA_HEADS, GLA_DK, GLA_DV), 1.0)
    inp['cache_ckv'] = nrm((n_pool, PAGE_SIZE, KV_LORA), 1.0)
    inp['cache_kpe'] = nrm((n_pool, PAGE_SIZE, QK_ROPE), 1.0)
    inp['page_table'] = jax.random.permutation(next(ks), n_pool)[:DEC_BATCH * n_pages].reshape(DEC_BATCH, n_pages).astype(jnp.int32)
    inp['g_norm_mix0'] = gain((D_MODEL,))
    inp['w_in0'] = nrm((D_MODEL, IN_W), D_MODEL ** -0.5)
    inp['w_gla_gate'] = nrm((GLA_GATE_RANK, GLA_QK_W), GLA_GATE_RANK ** -0.5)
    inp['b_gla_gate'] = nrm((GLA_QK_W,), 0.1)
    inp['g_ret_head'] = gain((RET_V_W,))
    inp['g_gla_head'] = gain((GLA_V_W,))
    inp['w_out0'] = nrm((MIX_W, D_MODEL), MIX_W ** -0.5)
    inp['g_norm_ffn0'] = gain((D_MODEL,))
    inp['w_ffn_gate'] = nrm((D_MODEL, D_FF), D_MODEL ** -0.5)
    inp['w_ffn_up'] = nrm((D_MODEL, D_FF), D_MODEL ** -0.5)
    inp['w_ffn_down'] = nrm((D_FF, D_MODEL), D_FF ** -0.5)
    inp['g_norm_mix1'] = gain((D_MODEL,))
    inp['w_dq'] = nrm((D_MODEL, Q_LORA), D_MODEL ** -0.5)
    inp['g_q'] = gain((Q_LORA,))
    inp['w_uq'] = nrm((Q_LORA, MLA_HEADS * (QK_NOPE + QK_ROPE)), Q_LORA ** -0.5)
    inp['w_dkv'] = nrm((D_MODEL, KV_LORA + QK_ROPE), D_MODEL ** -0.5)
    inp['g_kv'] = gain((KV_LORA,))
    inp['w_uk'] = nrm((KV_LORA, MLA_HEADS, QK_NOPE), KV_LORA ** -0.5)
    inp['w_uv'] = nrm((KV_LORA, MLA_HEADS, V_HEAD), KV_LORA ** -0.5)
    inp['w_o_mla'] = nrm((MLA_HEADS * V_HEAD, D_MODEL), (MLA_HEADS * V_HEAD) ** -0.5)
    inp['g_norm_ffn1'] = gain((D_MODEL,))
    inp['w_router'] = nrm((D_MODEL, N_EXPERTS), D_MODEL ** -0.5)
    inp['w_exp_gate'] = nrm((N_EXPERTS, D_MODEL, D_FF_EXPERT), D_MODEL ** -0.5)
    inp['w_exp_up'] = nrm((N_EXPERTS, D_MODEL, D_FF_EXPERT), D_MODEL ** -0.5)
    inp['w_exp_down'] = nrm((N_EXPERTS, D_FF_EXPERT, D_MODEL), D_FF_EXPERT ** -0.5)
    inp['g_final'] = gain((D_MODEL,))
    return inp


def reference(x_prompt, x_sample, state_ret, state_gla, cache_ckv, cache_kpe, page_table,
              g_norm_mix0, w_in0, w_gla_gate, b_gla_gate, g_ret_head, g_gla_head, w_out0,
              g_norm_ffn0, w_ffn_gate, w_ffn_up, w_ffn_down,
              g_norm_mix1, w_dq, g_q, w_uq, w_dkv, g_kv, w_uk, w_uv, w_o_mla,
              g_norm_ffn1, w_router, w_exp_gate, w_exp_up, w_exp_down, g_final):
    pos_p = jnp.arange(x_prompt.shape[1])
    pos_s = PAST_LEN + jnp.arange(x_sample.shape[1])
    xp, xs = x_prompt, x_sample
    for layer in range(DEPTH):
        if layer % 2 == 0:
            zero_ret = jnp.zeros((xp.shape[0], RET_HEADS, RET_DK, RET_DV), jnp.float32)
            zero_gla = jnp.zeros((xp.shape[0], GLA_HEADS, GLA_DK, GLA_DV), jnp.float32)
            yp, ret_p, gla_p = parallel_retention_gla(rms_norm(xp, g_norm_mix0), pos_p, zero_ret, zero_gla,
                                                      w_in0, w_gla_gate, b_gla_gate, g_ret_head, g_gla_head, w_out0)
            ys, ret_s, gla_s = parallel_retention_gla(rms_norm(xs, g_norm_mix0), pos_s, state_ret, state_gla,
                                                      w_in0, w_gla_gate, b_gla_gate, g_ret_head, g_gla_head, w_out0)
            xp = xp + yp
            xs = xs + ys
            xp = xp + swiglu(rms_norm(xp, g_norm_ffn0), w_ffn_gate, w_ffn_up, w_ffn_down)
            xs = xs + swiglu(rms_norm(xs, g_norm_ffn0), w_ffn_gate, w_ffn_up, w_ffn_down)
        else:
            n_pages = page_table.shape[1]
            past_ckv = cache_ckv[page_table].reshape(xs.shape[0], n_pages * PAGE_SIZE, KV_LORA)
            past_kpe = cache_kpe[page_table].reshape(xs.shape[0], n_pages * PAGE_SIZE, QK_ROPE)
            yp, ckv_p, kpe_p = mla_mixer(rms_norm(xp, g_norm_mix1), pos_p, None, None,
                                         w_dq, g_q, w_uq, w_dkv, g_kv, w_uk, w_uv, w_o_mla)
            ys, ckv_s, kpe_s = mla_mixer(rms_norm(xs, g_norm_mix1), pos_s, past_ckv, past_kpe,
                                         w_dq, g_q, w_uq, w_dkv, g_kv, w_uk, w_uv, w_o_mla)
            xp = xp + yp
            xs = xs + ys
            xp = xp + moe_swiglu(rms_norm(xp, g_norm_ffn1), w_router, w_exp_gate, w_exp_up, w_exp_down)
            xs = xs + moe_swiglu(rms_norm(xs, g_norm_ffn1), w_router, w_exp_gate, w_exp_up, w_exp_down)
    y_prompt = rms_norm(xp, g_final)
    y_sample = rms_norm(xs, g_final)
    return (y_prompt, y_sample, ret_p, gla_p, ckv_p, kpe_p, ret_s, gla_s, ckv_s, kpe_s)
```

```python
import functools
import math

import numpy as np
import jax
import jax.numpy as jnp
from jax import lax
from jax.experimental import pallas as pl
from jax.experimental.pallas import tpu as pltpu

F32 = jnp.float32
BF16 = jnp.bfloat16

D_MODEL = 1024
PAST_LEN = 8192
PAGE_SIZE = 128
N_HEADS_MIX = 4
DK = 128
DV = 256
GLA_GATE_RANK = 16
GLA_TAU = 16.0
QK_W = N_HEADS_MIX * DK
V_W = N_HEADS_MIX * DV
IN_W = 4 * QK_W + 4 * V_W + GLA_GATE_RANK
IN_W_PAD = 4 * QK_W + 4 * V_W + 128
D_FF = 2816
MLA_HEADS = 16
QK_NOPE = 64
QK_ROPE = 32
V_HEAD = 64
Q_LORA = 384
KV_LORA = 256
MLA_SCALE = (QK_NOPE + QK_ROPE) ** -0.5
N_EXPERTS = 8
D_FF_EXPERT = 3584
ROPE_THETA = 10000.0
EPS = 1e-6

OFF_RQ, OFF_RK, OFF_RV, OFF_RG = 0, QK_W, 2 * QK_W, 2 * QK_W + V_W
OFF_GQ = 2 * QK_W + 2 * V_W
OFF_GK = OFF_GQ + QK_W
OFF_GV = OFF_GK + QK_W
OFF_GR = OFF_GV + V_W
OFF_GA = OFF_GR + V_W

LOG_GAMMA = tuple(math.log1p(-2.0 ** (-5.0 - h)) for h in range(N_HEADS_MIX))
GLA_SAFE_LOG_DECAY = -80.0

ATT_TILE = 256
ATT_HEAD_GROUP = 4


def _cparams(sem, vmem_mb):
    return pltpu.CompilerParams(dimension_semantics=sem, vmem_limit_bytes=vmem_mb << 20)


def _rms(x, g):
    return x * lax.rsqrt(jnp.mean(x * x, axis=-1, keepdims=True) + EPS) * g


def _dot(a, b):
    return jnp.dot(a, b, preferred_element_type=F32)


def _dot_nt(a, b):
    return lax.dot_general(a, b, (((1,), (1,)), ((), ())), preferred_element_type=F32)


def _dot_tn(a, b):
    return lax.dot_general(a, b, (((0,), (0,)), ((), ())), preferred_element_type=F32)


def _silu(x):
    return x / (1.0 + jnp.exp(-x))


def _pad_rows(x, n):
    if x.shape[0] == n:
        return x
    return jnp.concatenate([x, jnp.zeros((n - x.shape[0], x.shape[1]), x.dtype)], axis=0)


def _norm_matmul_kernel(x_ref, g_ref, w_ref, o_ref, h_scr):
    @pl.when(pl.program_id(1) == 0)
    def _():
        h_scr[...] = _rms(x_ref[...], g_ref[...]).astype(BF16)

    o_ref[...] = _dot(h_scr[...], w_ref[...]).astype(o_ref.dtype)


def _norm_matmul(x, g, w, out_dtype, tm, tn):
    m, d = x.shape
    n = w.shape[1]
    return pl.pallas_call(
        _norm_matmul_kernel,
        out_shape=jax.ShapeDtypeStruct((m, n), out_dtype),
        grid=(m // tm, n // tn),
        in_specs=[pl.BlockSpec((tm, d), lambda i, j: (i, 0)),
                  pl.BlockSpec((1, d), lambda i, j: (0, 0)),
                  pl.BlockSpec((d, tn), lambda i, j: (0, j))],
        out_specs=pl.BlockSpec((tm, tn), lambda i, j: (i, j)),
        scratch_shapes=[pltpu.VMEM((tm, d), BF16)],
        compiler_params=_cparams(("parallel", "arbitrary"), 48),
        name="norm_in_proj",
    )(x, g, w)


def _cumsum_rows(x, n):
    row = lax.broadcasted_iota(jnp.int32, x.shape, 0)
    s = 1
    while s < n:
        x = x + jnp.where(row >= s, pltpu.roll(x, s, 0), 0.0)
        s *= 2
    return x


def _mixer_kernel(*refs, cq, ck, has_state, mm_dtype):
    if has_state:
        (proj, cos, sin, wgate, bgate, g_ret, g_gla, s0r, s0g,
         o_ref, ro_ref, go_ref, sr, sg, kf_scr, bc_scr, vf_scr) = refs
    else:
        (proj, cos, sin, wgate, bgate, g_ret, g_gla,
         o_ref, ro_ref, go_ref, sr, sg, kf_scr, bc_scr, vf_scr) = refs
    t = pl.program_id(1)

    @pl.when(t == 0)
    def _init():
        for h in range(N_HEADS_MIX):
            if has_state:
                sr[h] = s0r[0, h]
                sg[h] = s0g[0, h].T
            else:
                sr[h] = jnp.zeros((DK, DV), F32)
                sg[h] = jnp.zeros((DV, DK), F32)

    row = lax.broadcasted_iota(jnp.int32, (cq, ck), 0)
    col = lax.broadcasted_iota(jnp.int32, (cq, ck), 1)
    causal = row >= col
    relf = jnp.maximum(row - col, 0).astype(F32)
    ridx = lax.broadcasted_iota(jnp.int32, (cq, 1), 0).astype(F32)
    cosv = cos[...]
    sinv = sin[...]

    def rope(x):
        return x * cosv + pltpu.roll(x, DK // 2, 1) * sinv

    def col_slice(off, h, w):
        return proj[0, :, off + h * w: off + (h + 1) * w]

    for h in range(N_HEADS_MIX):
        lg = LOG_GAMMA[h]
        dmask = jnp.where(causal, jnp.exp(lg * relf), 0.0)
        qdec = jnp.exp(lg * (ridx + 1.0))
        kdec = jnp.exp(lg * (cq - 1.0 - ridx))
        cdec = math.exp(lg * cq)
        q = rope(col_slice(OFF_RQ, h, DK).astype(F32)).astype(mm_dtype)
        k = rope(col_slice(OFF_RK, h, DK).astype(F32)) * (DK ** -0.5)
        v = _pad_rows(col_slice(OFF_RV, h, DV).astype(mm_dtype), ck)
        a = _dot_nt(q, _pad_rows(k.astype(mm_dtype), ck)) * dmask
        s_old = sr[h]
        o = _dot(a.astype(mm_dtype), v) + _dot(q, s_old.astype(mm_dtype)) * qdec
        sr[h] = s_old * cdec + _dot_tn(_pad_rows((k * kdec).astype(mm_dtype), ck), v)
        oc = o - jnp.mean(o, axis=-1, keepdims=True)
        on = oc * lax.rsqrt(jnp.mean(oc * oc, axis=-1, keepdims=True) + EPS) * g_ret[h]
        gate = col_slice(OFF_RG, h, DV).astype(F32)
        o_ref[0, :, h * DV:(h + 1) * DV] = (on * _silu(gate)).astype(o_ref.dtype)

        ga = proj[0, :, OFF_GA:OFF_GA + 128].astype(BF16)
        z = _dot(ga, wgate[h]) + bgate[h]
        g = (jnp.minimum(z, 0.0) - jnp.log1p(jnp.exp(-jnp.abs(z)))) * (1.0 / GLA_TAU)
        bc = _cumsum_rows(g, cq)
        blast = bc[cq - 1:cq, :]
        qs = col_slice(OFF_GQ, h, DK).astype(F32) * (DK ** -0.5)
        kf = col_slice(OFF_GK, h, DK).astype(F32)
        vraw = col_slice(OFF_GV, h, DV)
        qe = (qs * jnp.exp(bc)).astype(mm_dtype)
        vp = _pad_rows(vraw.astype(mm_dtype), ck)

        def intra_fast():
            kinv = _pad_rows((kf * jnp.exp(-bc)).astype(mm_dtype), ck)
            af = jnp.where(causal, _dot_nt(qe, kinv), 0.0)
            return _dot(af.astype(mm_dtype), vp)

        def intra_exact():
            kf_scr[...] = kf
            bc_scr[...] = bc
            vf_scr[...] = vraw.astype(F32)

            def body(j, acc):
                kj = kf_scr[pl.ds(j, 1), :]
                bj = bc_scr[pl.ds(j, 1), :]
                vj = vf_scr[pl.ds(j, 1), :]
                w = jnp.exp(jnp.minimum(bc - bj, 0.0))
                c = jnp.sum(qs * kj * w, axis=1, keepdims=True)
                c = jnp.where(ridx >= j.astype(F32), c, 0.0)
                return acc + c * vj

            return lax.fori_loop(0, cq, body, jnp.zeros((cq, DV), F32))

        o_intra = lax.cond(jnp.min(blast) >= GLA_SAFE_LOG_DECAY, intra_fast, intra_exact)
        st_old = sg[h]
        og = o_intra + _dot_nt(qe, st_old.astype(mm_dtype))
        kk = _pad_rows((kf * jnp.exp(blast - bc)).astype(mm_dtype), ck)
        sg[h] = st_old * jnp.exp(blast) + _dot_tn(vp, kk)
        ogn = og * lax.rsqrt(jnp.mean(og * og, axis=-1, keepdims=True) + EPS) * g_gla[h]
        gateg = col_slice(OFF_GR, h, DV).astype(F32)
        o_ref[0, :, V_W + h * DV:V_W + (h + 1) * DV] = (ogn * _silu(gateg)).astype(o_ref.dtype)

    @pl.when(t == pl.num_programs(1) - 1)
    def _fin():
        for h in range(N_HEADS_MIX):
            ro_ref[0, h] = sr[h]
            go_ref[0, h] = sg[h].T


def _mixer(proj, cos, sin, wgate, bgate, g_ret, g_gla, s0_ret, s0_gla, *, chunk, out_dtype):
    b, t, w = proj.shape
    cq = chunk
    ck = max(chunk, 128)
    has_state = s0_ret is not None
    mm_dtype = BF16 if chunk >= 128 else F32
    const = lambda *shape: pl.BlockSpec(shape, lambda i, j: (0,) * len(shape))
    state_spec = pl.BlockSpec((1, N_HEADS_MIX, DK, DV), lambda i, j: (i, 0, 0, 0))
    in_specs = [pl.BlockSpec((1, cq, w), lambda i, j: (i, j, 0)),
                pl.BlockSpec((cq, DK), lambda i, j: (j, 0)),
                pl.BlockSpec((cq, DK), lambda i, j: (j, 0)),
                const(N_HEADS_MIX, 128, DK), const(N_HEADS_MIX, 1, DK),
                const(N_HEADS_MIX, 1, DV), const(N_HEADS_MIX, 1, DV)]
    args = [proj, cos, sin, wgate, bgate, g_ret, g_gla]
    if has_state:
        in_specs += [state_spec, state_spec]
        args += [s0_ret, s0_gla]
    return pl.pallas_call(
        functools.partial(_mixer_kernel, cq=cq, ck=ck, has_state=has_state, mm_dtype=mm_dtype),
        out_shape=(jax.ShapeDtypeStruct((b, t, 2 * V_W), out_dtype),
                   jax.ShapeDtypeStruct((b, N_HEADS_MIX, DK, DV), F32),
                   jax.ShapeDtypeStruct((b, N_HEADS_MIX, DK, DV), F32)),
        grid=(b, t // cq),
        in_specs=in_specs,
        out_specs=(pl.BlockSpec((1, cq, 2 * V_W), lambda i, j: (i, j, 0)), state_spec, state_spec),
        scratch_shapes=[pltpu.VMEM((N_HEADS_MIX, DK, DV), F32), pltpu.VMEM((N_HEADS_MIX, DV, DK), F32),
                        pltpu.VMEM((cq, DK), F32), pltpu.VMEM((cq, DK), F32), pltpu.VMEM((cq, DV), F32)],
        compiler_params=_cparams(("parallel", "arbitrary"), 48),
        name="retention_gla_mixer",
    )(*args)


def _resid_matmul_kernel(x_ref, a_ref, w_ref, o_ref):
    o_ref[...] = x_ref[...] + _dot(a_ref[...].astype(BF16), w_ref[...])


def _resid_matmul(x, a, w, tm):
    m, d = x.shape
    k = a.shape[1]
    return pl.pallas_call(
        _resid_matmul_kernel,
        out_shape=jax.ShapeDtypeStruct((m, d), F32),
        grid=(m // tm,),
        in_specs=[pl.BlockSpec((tm, d), lambda i: (i, 0)),
                  pl.BlockSpec((tm, k), lambda i: (i, 0)),
                  pl.BlockSpec((k, d), lambda i: (0, 0))],
        out_specs=pl.BlockSpec((tm, d), lambda i: (i, 0)),
        compiler_params=_cparams(("parallel",), 48),
        name="resid_out_proj",
    )(x, a, w)


def _ffn_kernel(x_ref, g_ref, wg_ref, wu_ref, wd_ref, o_ref, h_scr, acc_scr):
    f = pl.program_id(1)

    @pl.when(f == 0)
    def _():
        x = x_ref[...]
        h_scr[...] = _rms(x, g_ref[...]).astype(BF16)
        acc_scr[...] = x

    h = h_scr[...]
    act = _silu(_dot(h, wg_ref[...])) * _dot(h, wu_ref[...])
    acc_scr[...] += _dot(act.astype(BF16), wd_ref[...])

    @pl.when(f == pl.num_programs(1) - 1)
    def _():
        o_ref[...] = acc_scr[...]


def _ffn(x, g, wg, wu, wd, tm, tf):
    m, d = x.shape
    f = wg.shape[1]
    return pl.pallas_call(
        _ffn_kernel,
        out_shape=jax.ShapeDtypeStruct((m, d), F32),
        grid=(m // tm, f // tf),
        in_specs=[pl.BlockSpec((tm, d), lambda i, j: (i, 0)),
                  pl.BlockSpec((1, d), lambda i, j: (0, 0)),
                  pl.BlockSpec((d, tf), lambda i, j: (0, j)),
                  pl.BlockSpec((d, tf), lambda i, j: (0, j)),
                  pl.BlockSpec((tf, d), lambda i, j: (j, 0))],
        out_specs=pl.BlockSpec((tm, d), lambda i, j: (i, 0)),
        scratch_shapes=[pltpu.VMEM((tm, d), BF16), pltpu.VMEM((tm, d), F32)],
        compiler_params=_cparams(("parallel", "arbitrary"), 48),
        name="swiglu_ffn",
    )(x, g, wg, wu, wd)


def _mla_prep_kernel(x_ref, g1_ref, wdq_ref, gq_ref, wn_ref, wp_ref, wpr_ref, wuk_ref,
                     wkc_ref, wkp_ref, wkpr_ref, gkv_ref, cos_ref, sin_ref,
                     qlat_ref, qpe_ref, ckv_ref, kpe_ref):
    hn = _rms(x_ref[...], g1_ref[...]).astype(BF16)
    cq = _rms(_dot(hn, wdq_ref[...]), gq_ref[...]).astype(BF16)
    cosv = cos_ref[...]
    sinv = sin_ref[...]
    ckv_ref[...] = _rms(_dot(hn, wkc_ref[...]), gkv_ref[...])
    kpe_ref[...] = _dot(hn, wkp_ref[...]) * cosv + _dot(hn, wkpr_ref[...]) * sinv
    for h in range(MLA_HEADS):
        qn = _dot(cq, wn_ref[h]).astype(BF16)
        qlat_ref[0, h] = _dot(qn, wuk_ref[h]).astype(qlat_ref.dtype)
        qpe_ref[0, h] = (_dot(cq, wp_ref[h]) * cosv + _dot(cq, wpr_ref[h]) * sinv).astype(qpe_ref.dtype)


def _mla_prep(x, g1, wdq, gq, wn, wp, wpr, wuk, wkc, wkp, wkpr, gkv, cos, sin, q_dtype):
    m, d = x.shape
    tm = ATT_TILE
    nt = m // tm
    n_pos_tiles = cos.shape[0] // tm
    full = lambda a: pl.BlockSpec(a.shape, lambda i: (0,) * a.ndim)
    return pl.pallas_call(
        _mla_prep_kernel,
        out_shape=(jax.ShapeDtypeStruct((nt, MLA_HEADS, tm, KV_LORA), q_dtype),
                   jax.ShapeDtypeStruct((nt, MLA_HEADS, tm, QK_ROPE), q_dtype),
                   jax.ShapeDtypeStruct((m, KV_LORA), F32),
                   jax.ShapeDtypeStruct((m, QK_ROPE), F32)),
        grid=(nt,),
        in_specs=[pl.BlockSpec((tm, d), lambda i: (i, 0)), full(g1), full(wdq), full(gq), full(wn), full(wp),
                  full(wpr), full(wuk), full(wkc), full(wkp), full(wkpr), full(gkv),
                  pl.BlockSpec((tm, QK_ROPE), lambda i: (i % n_pos_tiles, 0)),
                  pl.BlockSpec((tm, QK_ROPE), lambda i: (i % n_pos_tiles, 0))],
        out_specs=(pl.BlockSpec((1, MLA_HEADS, tm, KV_LORA), lambda i: (i, 0, 0, 0)),
                   pl.BlockSpec((1, MLA_HEADS, tm, QK_ROPE), lambda i: (i, 0, 0, 0)),
                   pl.BlockSpec((tm, KV_LORA), lambda i: (i, 0)),
                   pl.BlockSpec((tm, QK_ROPE), lambda i: (i, 0))),
        compiler_params=_cparams(("parallel",), 48),
        name="mla_prep",
    )(x, g1, wdq, gq, wn, wp, wpr, wuk, wkc, wkp, wkpr, gkv, cos, sin)


def _attn_prompt_kernel(it_ref, jt_ref, q_ref, qpe_ref, ckv_ref, kpe_ref, o_ref,
                        kc_scr, kp_scr, m_scr, l_scr, acc_scr):
    p = pl.program_id(1)
    i = it_ref[p]
    j = jt_ref[p]
    tq = ATT_TILE
    hb = ATT_HEAD_GROUP
    rows = hb * tq

    @pl.when(j == 0)
    def _():
        m_scr[...] = jnp.full(m_scr.shape, -jnp.inf, F32)
        l_scr[...] = jnp.zeros(l_scr.shape, F32)
        acc_scr[...] = jnp.zeros(acc_scr.shape, F32)

    kc_scr[...] = ckv_ref[0].astype(BF16)
    kp_scr[...] = kpe_ref[0].astype(BF16)
    qpos = (lax.broadcasted_iota(jnp.int32, (rows, tq), 0) & (tq - 1)) + (i - j) * tq
    visible = lax.broadcasted_iota(jnp.int32, (rows, tq), 1) <= qpos

    def body(gidx, carry):
        hs = pl.ds(gidx * hb, hb)
        q = q_ref[0, hs].reshape(rows, KV_LORA)
        qp = qpe_ref[0, hs].reshape(rows, QK_ROPE)
        kc = kc_scr[...]
        s = (_dot_nt(q, kc) + _dot_nt(qp, kp_scr[...])) * MLA_SCALE
        s = jnp.where(visible, s, -jnp.inf)
        m_prev = m_scr[hs].reshape(rows, 1)
        m_new = jnp.maximum(m_prev, jnp.max(s, axis=1, keepdims=True))
        alpha = jnp.exp(m_prev - m_new)
        pr = jnp.exp(s - m_new)
        l_scr[hs] = (alpha * l_scr[hs].reshape(rows, 1) + jnp.sum(pr, axis=1, keepdims=True)).reshape(hb, tq, 1)
        m_scr[hs] = m_new.reshape(hb, tq, 1)
        acc = acc_scr[hs].reshape(rows, KV_LORA) * alpha + _dot(pr.astype(BF16), kc)
        acc_scr[hs] = acc.reshape(hb, tq, KV_LORA)
        return carry

    lax.fori_loop(0, MLA_HEADS // hb, body, 0)

    @pl.when(j == i)
    def _():
        o_ref[0] = (acc_scr[...] / l_scr[...]).astype(o_ref.dtype)


def _attn_prompt(q_lat, q_pe, ckv, kpe, batch, seq):
    tq = ATT_TILE
    nq = seq // tq
    pairs = [(i, j) for i in range(nq) for j in range(i + 1)]
    i_tab = jnp.asarray(np.array([p[0] for p in pairs], np.int32))
    j_tab = jnp.asarray(np.array([p[1] for p in pairs], np.int32))
    ckv3 = ckv.reshape(batch, seq, KV_LORA)
    kpe3 = kpe.reshape(batch, seq, QK_ROPE)
    return pl.pallas_call(
        _attn_prompt_kernel,
        out_shape=jax.ShapeDtypeStruct(q_lat.shape, BF16),
        grid_spec=pltpu.PrefetchScalarGridSpec(
            num_scalar_prefetch=2,
            grid=(batch, len(pairs)),
            in_specs=[pl.BlockSpec((1, MLA_HEADS, tq, KV_LORA), lambda b, p, it, jt: (b * nq + it[p], 0, 0, 0)),
                      pl.BlockSpec((1, MLA_HEADS, tq, QK_ROPE), lambda b, p, it, jt: (b * nq + it[p], 0, 0, 0)),
                      pl.BlockSpec((1, tq, KV_LORA), lambda b, p, it, jt: (b, jt[p], 0)),
                      pl.BlockSpec((1, tq, QK_ROPE), lambda b, p, it, jt: (b, jt[p], 0))],
            out_specs=pl.BlockSpec((1, MLA_HEADS, tq, KV_LORA), lambda b, p, it, jt: (b * nq + it[p], 0, 0, 0)),
            scratch_shapes=[pltpu.VMEM((tq, KV_LORA), BF16), pltpu.VMEM((tq, QK_ROPE), BF16),
                            pltpu.VMEM((MLA_HEADS, tq, 1), F32), pltpu.VMEM((MLA_HEADS, tq, 1), F32),
                            pltpu.VMEM((MLA_HEADS, tq, KV_LORA), F32)]),
        compiler_params=_cparams(("parallel", "arbitrary"), 48),
        name="mla_prompt_attention",
    )(i_tab, j_tab, q_lat, q_pe, ckv3, kpe3)


def _attn_sample_kernel(*refs, n_pages_step, dec_seq):
    g = n_pages_step
    pt_ref, q_ref, qpe_ref, cn_ref, pn_ref = refs[:5]
    c_refs = refs[5:5 + g]
    p_refs = refs[5 + g:5 + 2 * g]
    o_ref, m_scr, l_scr, acc_scr = refs[5 + 2 * g:]
    del pt_ref
    s_idx = pl.program_id(1)
    rows = MLA_HEADS * dec_seq
    q = q_ref[...].reshape(rows, KV_LORA).astype(BF16)
    qp = qpe_ref[...].reshape(rows, QK_ROPE).astype(BF16)

    def update(kc, kp, mask):
        s = (_dot_nt(q, kc) + _dot_nt(qp, kp)) * MLA_SCALE
        if mask is not None:
            s = jnp.where(mask, s, -jnp.inf)
        m_prev = m_scr[...]
        m_new = jnp.maximum(m_prev, jnp.max(s, axis=1, keepdims=True))
        alpha = jnp.exp(m_prev - m_new)
        pr = jnp.exp(s - m_new)
        l_scr[...] = alpha * l_scr[...] + jnp.sum(pr, axis=1, keepdims=True)
        m_scr[...] = m_new
        acc_scr[...] = acc_scr[...] * alpha + _dot(pr.astype(BF16), kc)

    @pl.when(s_idx == 0)
    def _():
        m_scr[...] = jnp.full(m_scr.shape, -jnp.inf, F32)
        l_scr[...] = jnp.zeros(l_scr.shape, F32)
        acc_scr[...] = jnp.zeros(acc_scr.shape, F32)
        kc = _pad_rows(cn_ref[0], 128).astype(BF16)
        kp = _pad_rows(pn_ref[0], 128).astype(BF16)
        tpos = lax.broadcasted_iota(jnp.int32, (rows, 128), 0) & (dec_seq - 1)
        update(kc, kp, lax.broadcasted_iota(jnp.int32, (rows, 128), 1) <= tpos)

    kc = jnp.concatenate([r[0] for r in c_refs], axis=0).astype(BF16)
    kp = jnp.concatenate([r[0] for r in p_refs], axis=0).astype(BF16)
    update(kc, kp, None)

    @pl.when(s_idx == pl.num_programs(1) - 1)
    def _():
        o_ref[...] = (acc_scr[...] / l_scr[...]).reshape(o_ref.shape).astype(o_ref.dtype)


def _attn_sample(q_lat, q_pe, ckv_new, kpe_new, cache_ckv, cache_kpe, page_table, dec_batch, dec_seq,
                 n_pages_step=8):
    tq = ATT_TILE
    nt = q_lat.shape[0]
    bpt = tq // dec_seq
    n_pages = page_table.shape[1]
    g = n_pages_step
    q5 = q_lat.reshape(nt, MLA_HEADS, bpt, dec_seq, KV_LORA)
    qp5 = q_pe.reshape(nt, MLA_HEADS, bpt, dec_seq, QK_ROPE)
    cn = ckv_new.reshape(dec_batch, dec_seq, KV_LORA)
    pn = kpe_new.reshape(dec_batch, dec_seq, QK_ROPE)
    q_spec = lambda w: pl.BlockSpec((1, MLA_HEADS, 1, dec_seq, w), lambda b, s, pt: (b // bpt, 0, b % bpt, 0, 0))
    page_spec = lambda w, gi: pl.BlockSpec((1, PAGE_SIZE, w),
                                           lambda b, s, pt: (pt[b * n_pages + s * g + gi], 0, 0))
    in_specs = ([q_spec(KV_LORA), q_spec(QK_ROPE),
                 pl.BlockSpec((1, dec_seq, KV_LORA), lambda b, s, pt: (b, 0, 0)),
                 pl.BlockSpec((1, dec_seq, QK_ROPE), lambda b, s, pt: (b, 0, 0))]
                + [page_spec(KV_LORA, gi) for gi in range(g)]
                + [page_spec(QK_ROPE, gi) for gi in range(g)])
    rows = MLA_HEADS * dec_seq
    out = pl.pallas_call(
        functools.partial(_attn_sample_kernel, n_pages_step=g, dec_seq=dec_seq),
        out_shape=jax.ShapeDtypeStruct(q5.shape, F32),
        grid_spec=pltpu.PrefetchScalarGridSpec(
            num_scalar_prefetch=1,
            grid=(dec_batch, n_pages // g),
            in_specs=in_specs,
            out_specs=q_spec(KV_LORA),
            scratch_shapes=[pltpu.VMEM((rows, 1), F32), pltpu.VMEM((rows, 1), F32),
                            pltpu.VMEM((rows, KV_LORA), F32)]),
        compiler_params=_cparams(("parallel", "arbitrary"), 48),
        name="mla_sample_attention",
    )(page_table.reshape(-1), q5, qp5, cn, pn, *([cache_ckv] * g), *([cache_kpe] * g))
    return out.reshape(q_lat.shape)


def _attn_out_kernel(x_ref, o_ref_in, wuv_ref, wo_ref, y_ref, cat_scr):
    for h in range(MLA_HEADS):
        cat_scr[:, h * V_HEAD:(h + 1) * V_HEAD] = _dot(o_ref_in[0, h].astype(BF16), wuv_ref[h])
    y_ref[...] = x_ref[...] + _dot(cat_scr[...].astype(BF16), wo_ref[...])


def _attn_out(x, o_lat, wuv, wo):
    m, d = x.shape
    tm = ATT_TILE
    return pl.pallas_call(
        _attn_out_kernel,
        out_shape=jax.ShapeDtypeStruct((m, d), F32),
        grid=(m // tm,),
        in_specs=[pl.BlockSpec((tm, d), lambda i: (i, 0)),
                  pl.BlockSpec((1, MLA_HEADS, tm, KV_LORA), lambda i: (i, 0, 0, 0)),
                  pl.BlockSpec(wuv.shape, lambda i: (0, 0, 0)),
                  pl.BlockSpec(wo.shape, lambda i: (0, 0))],
        out_specs=pl.BlockSpec((tm, d), lambda i: (i, 0)),
        scratch_shapes=[pltpu.VMEM((tm, MLA_HEADS * V_HEAD), F32)],
        compiler_params=_cparams(("parallel",), 48),
        name="mla_out_proj",
    )(x, o_lat, wuv, wo)


def _moe_kernel(x_ref, g_ref, wr_ref, wg_ref, wu_ref, wd_ref, gf_ref, o_ref, h_scr, gate_scr, acc_scr):
    e = pl.program_id(1)
    f = pl.program_id(2)
    tm = x_ref.shape[0]
    lane = lax.broadcasted_iota(jnp.int32, (tm, 128), 1).astype(F32)

    @pl.when((e == 0) & (f == 0))
    def _():
        x = x_ref[...]
        hn = _rms(x, g_ref[...])
        h_scr[...] = hn.astype(BF16)
        acc_scr[...] = x
        logits = jnp.dot(hn, wr_ref[...], preferred_element_type=F32, precision=lax.Precision.HIGHEST)
        logits = jnp.where(lane < N_EXPERTS, logits, -jnp.inf)
        m1 = jnp.max(logits, axis=1, keepdims=True)
        i1 = jnp.min(jnp.where(logits == m1, lane, 128.0), axis=1, keepdims=True)
        rest = jnp.where(lane == i1, -jnp.inf, logits)
        m2 = jnp.max(rest, axis=1, keepdims=True)
        i2 = jnp.min(jnp.where(rest == m2, lane, 128.0), axis=1, keepdims=True)
        e2 = jnp.exp(m2 - m1)
        w1 = 1.0 / (1.0 + e2)
        gate_scr[...] = jnp.where(lane == i1, w1, 0.0) + jnp.where(lane == i2, e2 * w1, 0.0)

    h = h_scr[...]
    gcol = jnp.sum(jnp.where(lane == e.astype(F32), gate_scr[...], 0.0), axis=1, keepdims=True)
    act = _silu(_dot(h, wg_ref[0])) * _dot(h, wu_ref[0]) * gcol
    acc_scr[...] += _dot(act.astype(BF16), wd_ref[0])

    @pl.when((e == pl.num_programs(1) - 1) & (f == pl.num_programs(2) - 1))
    def _():
        o_ref[...] = _rms(acc_scr[...], gf_ref[...])


def _moe(x, g, w_router_pad, wg, wu, wd, g_final, tm, tf):
    m, d = x.shape
    ne, _, f = wg.shape
    return pl.pallas_call(
        _moe_kernel,
        out_shape=jax.ShapeDtypeStruct((m, d), F32),
        grid=(m // tm, ne, f // tf),
        in_specs=[pl.BlockSpec((tm, d), lambda i, e, j: (i, 0)),
                  pl.BlockSpec((1, d), lambda i, e, j: (0, 0)),
                  pl.BlockSpec((d, 128), lambda i, e, j: (0, 0)),
                  pl.BlockSpec((1, d, tf), lambda i, e, j: (e, 0, j)),
                  pl.BlockSpec((1, d, tf), lambda i, e, j: (e, 0, j)),
                  pl.BlockSpec((1, tf, d), lambda i, e, j: (e, j, 0)),
                  pl.BlockSpec((1, d), lambda i, e, j: (0, 0))],
        out_specs=pl.BlockSpec((tm, d), lambda i, e, j: (i, 0)),
        scratch_shapes=[pltpu.VMEM((tm, d), BF16), pltpu.VMEM((tm, 128), F32), pltpu.VMEM((tm, d), F32)],
        compiler_params=_cparams(("parallel", "arbitrary", "arbitrary"), 56),
        name="moe_experts",
    )(x, g, w_router_pad, wg, wu, wd, g_final)


def _rope_tables(pos, dim, signed):
    half = dim // 2
    inv = ROPE_THETA ** (-jnp.arange(half, dtype=F32) / half)
    ang = pos.astype(F32)[:, None] * inv[None, :]
    c, s = jnp.cos(ang), jnp.sin(ang)
    return jnp.concatenate([c, c], axis=-1), jnp.concatenate([-s if signed else s, s], axis=-1)


def _rot_cols(w):
    half = w.shape[-1] // 2
    return jnp.concatenate([-w[..., half:], w[..., :half]], axis=-1)


def _prep_weights(w_in0, w_gla_gate, b_gla_gate, g_ret_head, g_gla_head, w_out0, w_ffn_gate, w_ffn_up,
                  w_ffn_down, w_dq, w_uq, w_dkv, w_uk, w_uv, w_o_mla, w_router, w_exp_gate, w_exp_up,
                  w_exp_down):
    p = {}
    p["w_in"] = jnp.pad(w_in0, ((0, 0), (0, IN_W_PAD - IN_W))).astype(BF16)
    wgate = w_gla_gate.reshape(GLA_GATE_RANK, N_HEADS_MIX, DK).transpose(1, 0, 2)
    p["wgate"] = jnp.pad(wgate, ((0, 0), (0, 128 - GLA_GATE_RANK), (0, 0))).astype(BF16)
    p["bgate"] = b_gla_gate.reshape(N_HEADS_MIX, 1, DK)
    p["g_ret"] = g_ret_head.reshape(N_HEADS_MIX, 1, DV)
    p["g_gla"] = g_gla_head.reshape(N_HEADS_MIX, 1, DV)
    p["w_out"] = w_out0.astype(BF16)
    p["w_ffn_gate"] = w_ffn_gate.astype(BF16)
    p["w_ffn_up"] = w_ffn_up.astype(BF16)
    p["w_ffn_down"] = w_ffn_down.astype(BF16)
    p["w_dq"] = w_dq.astype(BF16)
    wuq = w_uq.reshape(Q_LORA, MLA_HEADS, QK_NOPE + QK_ROPE).transpose(1, 0, 2)
    p["w_uq_nope"] = wuq[..., :QK_NOPE].astype(BF16)
    p["w_uq_pe"] = wuq[..., QK_NOPE:].astype(BF16)
    p["w_uq_pe_rot"] = _rot_cols(wuq[..., QK_NOPE:]).astype(BF16)
    p["w_uk_t"] = w_uk.transpose(1, 2, 0).astype(BF16)
    p["w_dkv_c"] = w_dkv[:, :KV_LORA].astype(BF16)
    p["w_dkv_pe"] = w_dkv[:, KV_LORA:].astype(BF16)
    p["w_dkv_pe_rot"] = _rot_cols(w_dkv[:, KV_LORA:]).astype(BF16)
    p["w_uv"] = w_uv.transpose(1, 0, 2).astype(BF16)
    p["w_o"] = w_o_mla.astype(BF16)
    p["w_router"] = jnp.pad(w_router, ((0, 0), (0, 128 - N_EXPERTS)))
    p["w_exp_gate"] = w_exp_gate.astype(BF16)
    p["w_exp_up"] = w_exp_up.astype(BF16)
    p["w_exp_down"] = w_exp_down.astype(BF16)
    return p


def _row(v):
    return v.reshape(1, -1)


def _trunk(x, pos, p, norms, *, batch, seq, chunk, s0_ret, s0_gla, attn_fn, act_dtype):
    m = batch * seq
    tm = min(m, 1024)
    cos_r, sin_r = _rope_tables(pos, DK, signed=True)
    proj = _norm_matmul(x, _row(norms["g_norm_mix0"]), p["w_in"], act_dtype, tm, IN_W_PAD // 7)
    mix, ret_s, gla_s = _mixer(proj.reshape(batch, seq, IN_W_PAD), cos_r, sin_r, p["wgate"], p["bgate"],
                               p["g_ret"], p["g_gla"], s0_ret, s0_gla, chunk=chunk, out_dtype=act_dtype)
    x = _resid_matmul(x, mix.reshape(m, 2 * V_W), p["w_out"], min(m, 512))
    x = _ffn(x, _row(norms["g_norm_ffn0"]), p["w_ffn_gate"], p["w_ffn_up"], p["w_ffn_down"], tm, 256)
    cos_m, sin_m = _rope_tables(pos, QK_ROPE, signed=False)
    reps = max(ATT_TILE // seq, 1)
    cos_m, sin_m = jnp.tile(cos_m, (reps, 1)), jnp.tile(sin_m, (reps, 1))
    q_lat, q_pe, ckv, kpe = _mla_prep(x, _row(norms["g_norm_mix1"]), p["w_dq"], _row(norms["g_q"]),
                                      p["w_uq_nope"], p["w_uq_pe"], p["w_uq_pe_rot"], p["w_uk_t"],
                                      p["w_dkv_c"], p["w_dkv_pe"], p["w_dkv_pe_rot"], _row(norms["g_kv"]),
                                      cos_m, sin_m, act_dtype)
    o_lat = attn_fn(q_lat, q_pe, ckv, kpe)
    x = _attn_out(x, o_lat, p["w_uv"], p["w_o"])
    y = _moe(x, _row(norms["g_norm_ffn1"]), p["w_router"], p["w_exp_gate"], p["w_exp_up"], p["w_exp_down"],
             _row(norms["g_final"]), tm, 512)
    return y, ret_s, gla_s, ckv, kpe


def kernel(x_prompt, x_sample, state_ret, state_gla, cache_ckv, cache_kpe, page_table, g_norm_mix0, w_in0, w_gla_gate, b_gla_gate, g_ret_head, g_gla_head, w_out0, g_norm_ffn0, w_ffn_gate, w_ffn_up, w_ffn_down, g_norm_mix1, w_dq, g_q, w_uq, w_dkv, g_kv, w_uk, w_uv, w_o_mla, g_norm_ffn1, w_router, w_exp_gate, w_exp_up, w_exp_down, g_final):
    batch, seq, _ = x_prompt.shape
    dec_batch, dec_seq, _ = x_sample.shape
    p = _prep_weights(w_in0, w_gla_gate, b_gla_gate, g_ret_head, g_gla_head, w_out0, w_ffn_gate, w_ffn_up,
                      w_ffn_down, w_dq, w_uq, w_dkv, w_uk, w_uv, w_o_mla, w_router, w_exp_gate, w_exp_up,
                      w_exp_down)
    norms = dict(g_norm_mix0=g_norm_mix0, g_norm_ffn0=g_norm_ffn0, g_norm_mix1=g_norm_mix1, g_q=g_q,
                 g_kv=g_kv, g_norm_ffn1=g_norm_ffn1, g_final=g_final)

    yp, ret_p, gla_p, ckv_p, kpe_p = _trunk(
        x_prompt.reshape(batch * seq, D_MODEL), jnp.arange(seq), p, norms,
        batch=batch, seq=seq, chunk=128, s0_ret=None, s0_gla=None,
        attn_fn=lambda ql, qp, c, k: _attn_prompt(ql, qp, c, k, batch, seq), act_dtype=BF16)

    ys, ret_s, gla_s, ckv_s, kpe_s = _trunk(
        x_sample.reshape(dec_batch * dec_seq, D_MODEL), PAST_LEN + jnp.arange(dec_seq), p, norms,
        batch=dec_batch, seq=dec_seq, chunk=dec_seq, s0_ret=state_ret, s0_gla=state_gla,
        attn_fn=lambda ql, qp, c, k: _attn_sample(ql, qp, c, k, cache_ckv, cache_kpe, page_table,
                                                  dec_batch, dec_seq),
        act_dtype=F32)

    return (yp.reshape(batch, seq, D_MODEL), ys.reshape(dec_batch, dec_seq, D_MODEL),
            ret_p, gla_p, ckv_p.reshape(batch, seq, KV_LORA), kpe_p.reshape(batch, seq, QK_ROPE),
            ret_s, gla_s, ckv_s.reshape(dec_batch, dec_seq, KV_LORA), kpe_s.reshape(dec_batch, dec_seq, QK_ROPE))
```

```python
import functools
import math

import numpy as np
import jax
import jax.numpy as jnp
from jax import lax
from jax.experimental import pallas as pl
from jax.experimental.pallas import tpu as pltpu

F32 = jnp.float32
BF16 = jnp.bfloat16

D_MODEL = 1024
PAST_LEN = 8192
PAGE_SIZE = 128
N_HEADS_MIX = 4
DK = 128
DV = 256
GLA_GATE_RANK = 16
GLA_TAU = 16.0
QK_W = N_HEADS_MIX * DK
V_W = N_HEADS_MIX * DV
IN_W = 4 * QK_W + 4 * V_W + GLA_GATE_RANK
IN_W_PAD = 4 * QK_W + 4 * V_W + 128
D_FF = 2816
MLA_HEADS = 16
QK_NOPE = 64
QK_ROPE = 32
V_HEAD = 64
Q_LORA = 384
KV_LORA = 256
MLA_SCALE = (QK_NOPE + QK_ROPE) ** -0.5
N_EXPERTS = 8
D_FF_EXPERT = 3584
ROPE_THETA = 10000.0
EPS = 1e-6

OFF_RQ, OFF_RK, OFF_RV, OFF_RG = 0, QK_W, 2 * QK_W, 2 * QK_W + V_W
OFF_GQ = 2 * QK_W + 2 * V_W
OFF_GK = OFF_GQ + QK_W
OFF_GV = OFF_GK + QK_W
OFF_GR = OFF_GV + V_W
OFF_GA = OFF_GR + V_W

LOG_GAMMA = tuple(math.log1p(-2.0 ** (-5.0 - h)) for h in range(N_HEADS_MIX))
GLA_SAFE_LOG_DECAY = -80.0

ATT_TILE = 256
ATT_HEAD_GROUP = 4


def _cparams(sem, vmem_mb):
    return pltpu.CompilerParams(dimension_semantics=sem, vmem_limit_bytes=vmem_mb << 20)


def _rms(x, g):
    return x * lax.rsqrt(jnp.mean(x * x, axis=-1, keepdims=True) + EPS) * g


def _dot(a, b):
    return jnp.dot(a, b, preferred_element_type=F32)


def _dot_nt(a, b):
    return lax.dot_general(a, b, (((1,), (1,)), ((), ())), preferred_element_type=F32)


def _dot_tn(a, b):
    return lax.dot_general(a, b, (((0,), (0,)), ((), ())), preferred_element_type=F32)


def _silu(x):
    return x / (1.0 + jnp.exp(-x))


def _pad_rows(x, n):
    if x.shape[0] == n:
        return x
    return jnp.concatenate([x, jnp.zeros((n - x.shape[0], x.shape[1]), x.dtype)], axis=0)


def _norm_matmul_kernel(x_ref, g_ref, w_ref, o_ref, h_scr):
    @pl.when(pl.program_id(1) == 0)
    def _():
        h_scr[...] = _rms(x_ref[...], g_ref[...]).astype(BF16)

    o_ref[...] = _dot(h_scr[...], w_ref[...]).astype(o_ref.dtype)


def _norm_matmul(x, g, w, out_dtype, tm, tn):
    m, d = x.shape
    n = w.shape[1]
    return pl.pallas_call(
        _norm_matmul_kernel,
        out_shape=jax.ShapeDtypeStruct((m, n), out_dtype),
        grid=(m // tm, n // tn),
        in_specs=[pl.BlockSpec((tm, d), lambda i, j: (i, 0)),
                  pl.BlockSpec((1, d), lambda i, j: (0, 0)),
                  pl.BlockSpec((d, tn), lambda i, j: (0, j))],
        out_specs=pl.BlockSpec((tm, tn), lambda i, j: (i, j)),
        scratch_shapes=[pltpu.VMEM((tm, d), BF16)],
        compiler_params=_cparams(("parallel", "arbitrary"), 48),
        name="norm_in_proj",
    )(x, g, w)


def _cumsum_rows(x, n):
    row = lax.broadcasted_iota(jnp.int32, x.shape, 0)
    s = 1
    while s < n:
        x = x + jnp.where(row >= s, pltpu.roll(x, s, 0), 0.0)
        s *= 2
    return x


def _mixer_kernel(*refs, cq, ck, has_state, mm_dtype):
    if has_state:
        (proj, cos, sin, wgate, bgate, g_ret, g_gla, s0r, s0g,
         o_ref, ro_ref, go_ref, sr, sg, kf_scr, bc_scr, vf_scr) = refs
    else:
        (proj, cos, sin, wgate, bgate, g_ret, g_gla,
         o_ref, ro_ref, go_ref, sr, sg, kf_scr, bc_scr, vf_scr) = refs
    t = pl.program_id(1)

    @pl.when(t == 0)
    def _init():
        for h in range(N_HEADS_MIX):
            if has_state:
                sr[h] = s0r[0, h]
                sg[h] = s0g[0, h].T
            else:
                sr[h] = jnp.zeros((DK, DV), F32)
                sg[h] = jnp.zeros((DV, DK), F32)

    row = lax.broadcasted_iota(jnp.int32, (cq, ck), 0)
    col = lax.broadcasted_iota(jnp.int32, (cq, ck), 1)
    causal = row >= col
    relf = jnp.maximum(row - col, 0).astype(F32)
    ridx = lax.broadcasted_iota(jnp.int32, (cq, 1), 0).astype(F32)
    cosv = cos[...]
    sinv = sin[...]

    def rope(x):
        return x * cosv + pltpu.roll(x, DK // 2, 1) * sinv

    def col_slice(off, h, w):
        return proj[0, :, off + h * w: off + (h + 1) * w]

    for h in range(N_HEADS_MIX):
        lg = LOG_GAMMA[h]
        dmask = jnp.where(causal, jnp.exp(lg * relf), 0.0)
        qdec = jnp.exp(lg * (ridx + 1.0))
        kdec = jnp.exp(lg * (cq - 1.0 - ridx))
        cdec = math.exp(lg * cq)
        q = rope(col_slice(OFF_RQ, h, DK).astype(F32)).astype(mm_dtype)
        k = rope(col_slice(OFF_RK, h, DK).astype(F32)) * (DK ** -0.5)
        v = _pad_rows(col_slice(OFF_RV, h, DV).astype(mm_dtype), ck)
        a = _dot_nt(q, _pad_rows(k.astype(mm_dtype), ck)) * dmask
        s_old = sr[h]
        o = _dot(a.astype(mm_dtype), v) + _dot(q, s_old.astype(mm_dtype)) * qdec
        sr[h] = s_old * cdec + _dot_tn(_pad_rows((k * kdec).astype(mm_dtype), ck), v)
        oc = o - jnp.mean(o, axis=-1, keepdims=True)
        on = oc * lax.rsqrt(jnp.mean(oc * oc, axis=-1, keepdims=True) + EPS) * g_ret[h]
        gate = col_slice(OFF_RG, h, DV).astype(F32)
        o_ref[0, :, h * DV:(h + 1) * DV] = (on * _silu(gate)).astype(o_ref.dtype)

        ga = proj[0, :, OFF_GA:OFF_GA + 128].astype(BF16)
        z = _dot(ga, wgate[h]) + bgate[h]
        g = (jnp.minimum(z, 0.0) - jnp.log1p(jnp.exp(-jnp.abs(z)))) * (1.0 / GLA_TAU)
        bc = _cumsum_rows(g, cq)
        blast = bc[cq - 1:cq, :]
        qs = col_slice(OFF_GQ, h, DK).astype(F32) * (DK ** -0.5)
        kf = col_slice(OFF_GK, h, DK).astype(F32)
        vraw = col_slice(OFF_GV, h, DV)
        qe = (qs * jnp.exp(bc)).astype(mm_dtype)
        vp = _pad_rows(vraw.astype(mm_dtype), ck)

        def intra_fast():
            kinv = _pad_rows((kf * jnp.exp(-bc)).astype(mm_dtype), ck)
            af = jnp.where(causal, _dot_nt(qe, kinv), 0.0)
            return _dot(af.astype(mm_dtype), vp)

        def intra_exact():
            kf_scr[...] = kf
            bc_scr[...] = bc
            vf_scr[...] = vraw.astype(F32)

            def body(j, acc):
                kj = kf_scr[pl.ds(j, 1), :]
                bj = bc_scr[pl.ds(j, 1), :]
                vj = vf_scr[pl.ds(j, 1), :]
                w = jnp.exp(jnp.minimum(bc - bj, 0.0))
                c = jnp.sum(qs * kj * w, axis=1, keepdims=True)
                c = jnp.where(ridx >= j.astype(F32), c, 0.0)
                return acc + c * vj

            return lax.fori_loop(0, cq, body, jnp.zeros((cq, DV), F32))

        o_intra = lax.cond(jnp.min(blast) >= GLA_SAFE_LOG_DECAY, intra_fast, intra_exact)
        st_old = sg[h]
        og = o_intra + _dot_nt(qe, st_old.astype(mm_dtype))
        kk = _pad_rows((kf * jnp.exp(blast - bc)).astype(mm_dtype), ck)
        sg[h] = st_old * jnp.exp(blast) + _dot_tn(vp, kk)
        ogn = og * lax.rsqrt(jnp.mean(og * og, axis=-1, keepdims=True) + EPS) * g_gla[h]
        gateg = col_slice(OFF_GR, h, DV).astype(F32)
        o_ref[0, :, V_W + h * DV:V_W + (h + 1) * DV] = (ogn * _silu(gateg)).astype(o_ref.dtype)

    @pl.when(t == pl.num_programs(1) - 1)
    def _fin():
        for h in range(N_HEADS_MIX):
            ro_ref[0, h] = sr[h]
            go_ref[0, h] = sg[h].T


def _mixer(proj, cos, sin, wgate, bgate, g_ret, g_gla, s0_ret, s0_gla, *, chunk, out_dtype):
    b, t, w = proj.shape
    cq = chunk
    ck = max(chunk, 128)
    has_state = s0_ret is not None
    mm_dtype = BF16 if chunk >= 128 else F32
    const = lambda *shape: pl.BlockSpec(shape, lambda i, j: (0,) * len(shape))
    state_spec = pl.BlockSpec((1, N_HEADS_MIX, DK, DV), lambda i, j: (i, 0, 0, 0))
    in_specs = [pl.BlockSpec((1, cq, w), lambda i, j: (i, j, 0)),
                pl.BlockSpec((cq, DK), lambda i, j: (j, 0)),
                pl.BlockSpec((cq, DK), lambda i, j: (j, 0)),
                const(N_HEADS_MIX, 128, DK), const(N_HEADS_MIX, 1, DK),
                const(N_HEADS_MIX, 1, DV), const(N_HEADS_MIX, 1, DV)]
    args = [proj, cos, sin, wgate, bgate, g_ret, g_gla]
    if has_state:
        in_specs += [state_spec, state_spec]
        args += [s0_ret, s0_gla]
    return pl.pallas_call(
        functools.partial(_mixer_kernel, cq=cq, ck=ck, has_state=has_state, mm_dtype=mm_dtype),
        out_shape=(jax.ShapeDtypeStruct((b, t, 2 * V_W), out_dtype),
                   jax.ShapeDtypeStruct((b, N_HEADS_MIX, DK, DV), F32),
                   jax.ShapeDtypeStruct((b, N_HEADS_MIX, DK, DV), F32)),
        grid=(b, t // cq),
        in_specs=in_specs,
        out_specs=(pl.BlockSpec((1, cq, 2 * V_W), lambda i, j: (i, j, 0)), state_spec, state_spec),
        scratch_shapes=[pltpu.VMEM((N_HEADS_MIX, DK, DV), F32), pltpu.VMEM((N_HEADS_MIX, DV, DK), F32),
                        pltpu.VMEM((cq, DK), F32), pltpu.VMEM((cq, DK), F32), pltpu.VMEM((cq, DV), F32)],
        compiler_params=_cparams(("parallel", "arbitrary"), 48),
        name="retention_gla_mixer",
    )(*args)


def _resid_matmul_kernel(x_ref, a_ref, w_ref, o_ref):
    o_ref[...] = x_ref[...] + _dot(a_ref[...].astype(BF16), w_ref[...])


def _resid_matmul(x, a, w, tm):
    m, d = x.shape
    k = a.shape[1]
    return pl.pallas_call(
        _resid_matmul_kernel,
        out_shape=jax.ShapeDtypeStruct((m, d), F32),
        grid=(m // tm,),
        in_specs=[pl.BlockSpec((tm, d), lambda i: (i, 0)),
                  pl.BlockSpec((tm, k), lambda i: (i, 0)),
                  pl.BlockSpec((k, d), lambda i: (0, 0))],
        out_specs=pl.BlockSpec((tm, d), lambda i: (i, 0)),
        compiler_params=_cparams(("parallel",), 48),
        name="resid_out_proj",
    )(x, a, w)


def _ffn_kernel(x_ref, g_ref, wg_ref, wu_ref, wd_ref, o_ref, h_scr, acc_scr):
    f = pl.program_id(1)

    @pl.when(f == 0)
    def _():
        x = x_ref[...]
        h_scr[...] = _rms(x, g_ref[...]).astype(BF16)
        acc_scr[...] = x

    h = h_scr[...]
    act = _silu(_dot(h, wg_ref[...])) * _dot(h, wu_ref[...])
    acc_scr[...] += _dot(act.astype(BF16), wd_ref[...])

    @pl.when(f == pl.num_programs(1) - 1)
    def _():
        o_ref[...] = acc_scr[...]


def _ffn(x, g, wg, wu, wd, tm, tf):
    m, d = x.shape
    f = wg.shape[1]
    return pl.pallas_call(
        _ffn_kernel,
        out_shape=jax.ShapeDtypeStruct((m, d), F32),
        grid=(m // tm, f // tf),
        in_specs=[pl.BlockSpec((tm, d), lambda i, j: (i, 0)),
                  pl.BlockSpec((1, d), lambda i, j: (0, 0)),
                  pl.BlockSpec((d, tf), lambda i, j: (0, j)),
                  pl.BlockSpec((d, tf), lambda i, j: (0, j)),
                  pl.BlockSpec((tf, d), lambda i, j: (j, 0))],
        out_specs=pl.BlockSpec((tm, d), lambda i, j: (i, 0)),
        scratch_shapes=[pltpu.VMEM((tm, d), BF16), pltpu.VMEM((tm, d), F32)],
        compiler_params=_cparams(("parallel", "arbitrary"), 48),
        name="swiglu_ffn",
    )(x, g, wg, wu, wd)


def _mla_prep_kernel(x_ref, g1_ref, wdq_ref, gq_ref, wn_ref, wp_ref, wpr_ref, wuk_ref,
                     wkc_ref, wkp_ref, wkpr_ref, gkv_ref, cos_ref, sin_ref,
                     qlat_ref, qpe_ref, ckv_ref, kpe_ref):
    hn = _rms(x_ref[...], g1_ref[...]).astype(BF16)
    cq = _rms(_dot(hn, wdq_ref[...]), gq_ref[...]).astype(BF16)
    cosv = cos_ref[...]
    sinv = sin_ref[...]
    ckv_ref[...] = _rms(_dot(hn, wkc_ref[...]), gkv_ref[...])
    kpe_ref[...] = _dot(hn, wkp_ref[...]) * cosv + _dot(hn, wkpr_ref[...]) * sinv
    for h in range(MLA_HEADS):
        qn = _dot(cq, wn_ref[h]).astype(BF16)
        qlat_ref[0, h] = _dot(qn, wuk_ref[h]).astype(qlat_ref.dtype)
        qpe_ref[0, h] = (_dot(cq, wp_ref[h]) * cosv + _dot(cq, wpr_ref[h]) * sinv).astype(qpe_ref.dtype)


def _mla_prep(x, g1, wdq, gq, wn, wp, wpr, wuk, wkc, wkp, wkpr, gkv, cos, sin, q_dtype):
    m, d = x.shape
    tm = ATT_TILE
    nt = m // tm
    n_pos_tiles = cos.shape[0] // tm
    full = lambda a: pl.BlockSpec(a.shape, lambda i: (0,) * a.ndim)
    return pl.pallas_call(
        _mla_prep_kernel,
        out_shape=(jax.ShapeDtypeStruct((nt, MLA_HEADS, tm, KV_LORA), q_dtype),
                   jax.ShapeDtypeStruct((nt, MLA_HEADS, tm, QK_ROPE), q_dtype),
                   jax.ShapeDtypeStruct((m, KV_LORA), F32),
                   jax.ShapeDtypeStruct((m, QK_ROPE), F32)),
        grid=(nt,),
        in_specs=[pl.BlockSpec((tm, d), lambda i: (i, 0)), full(g1), full(wdq), full(gq), full(wn), full(wp),
                  full(wpr), full(wuk), full(wkc), full(wkp), full(wkpr), full(gkv),
                  pl.BlockSpec((tm, QK_ROPE), lambda i: (i % n_pos_tiles, 0)),
                  pl.BlockSpec((tm, QK_ROPE), lambda i: (i % n_pos_tiles, 0))],
        out_specs=(pl.BlockSpec((1, MLA_HEADS, tm, KV_LORA), lambda i: (i, 0, 0, 0)),
                   pl.BlockSpec((1, MLA_HEADS, tm, QK_ROPE), lambda i: (i, 0, 0, 0)),
                   pl.BlockSpec((tm, KV_LORA), lambda i: (i, 0)),
                   pl.BlockSpec((tm, QK_ROPE), lambda i: (i, 0))),
        compiler_params=_cparams(("parallel",), 48),
        name="mla_prep",
    )(x, g1, wdq, gq, wn, wp, wpr, wuk, wkc, wkp, wkpr, gkv, cos, sin)


def _attn_prompt_kernel(it_ref, jt_ref, q_ref, qpe_ref, ckv_ref, kpe_ref, o_ref,
                        kc_scr, kp_scr, m_scr, l_scr, acc_scr):
    p = pl.program_id(1)
    i = it_ref[p]
    j = jt_ref[p]
    tq = ATT_TILE
    hb = ATT_HEAD_GROUP
    rows = hb * tq

    @pl.when(j == 0)
    def _():
        m_scr[...] = jnp.full(m_scr.shape, -jnp.inf, F32)
        l_scr[...] = jnp.zeros(l_scr.shape, F32)
        acc_scr[...] = jnp.zeros(acc_scr.shape, F32)

    kc_scr[...] = ckv_ref[0].astype(BF16)
    kp_scr[...] = kpe_ref[0].astype(BF16)

    def make_body(masked):
        def body(gidx, carry):
            hs = pl.ds(gidx * hb, hb)
            q = q_ref[0, hs].reshape(rows, KV_LORA)
            qp = qpe_ref[0, hs].reshape(rows, QK_ROPE)
            kc = kc_scr[...]
            s = (_dot_nt(q, kc) + _dot_nt(qp, kp_scr[...])) * MLA_SCALE
            if masked:
                qpos = lax.broadcasted_iota(jnp.int32, (rows, tq), 0) & (tq - 1)
                s = jnp.where(lax.broadcasted_iota(jnp.int32, (rows, tq), 1) <= qpos, s, -jnp.inf)
            m_prev = m_scr[hs].reshape(rows, 128)
            m_new = jnp.maximum(m_prev, jnp.max(s, axis=1, keepdims=True))
            alpha = jnp.exp(m_prev - m_new)
            pr = jnp.exp(s - pltpu.repeat(m_new, tq // 128, axis=1))
            l_new = alpha * l_scr[hs].reshape(rows, 128) + jnp.sum(pr, axis=1, keepdims=True)
            l_scr[hs] = l_new.reshape(hb, tq, 128)
            m_scr[hs] = m_new.reshape(hb, tq, 128)
            acc = (acc_scr[hs].reshape(rows, KV_LORA) * pltpu.repeat(alpha, KV_LORA // 128, axis=1)
                   + _dot(pr.astype(BF16), kc))
            acc_scr[hs] = acc.reshape(hb, tq, KV_LORA)
            return carry
        return body

    @pl.when(j < i)
    def _():
        lax.fori_loop(0, MLA_HEADS // hb, make_body(False), 0)

    @pl.when(j == i)
    def _():
        lax.fori_loop(0, MLA_HEADS // hb, make_body(True), 0)
        inv_l = 1.0 / l_scr[...].reshape(MLA_HEADS * tq, 128)
        acc = acc_scr[...].reshape(MLA_HEADS * tq, KV_LORA) * pltpu.repeat(inv_l, KV_LORA // 128, axis=1)
        o_ref[0] = acc.reshape(MLA_HEADS, tq, KV_LORA).astype(o_ref.dtype)


def _attn_prompt(q_lat, q_pe, ckv, kpe, batch, seq):
    tq = ATT_TILE
    nq = seq // tq
    pairs = [(i, j) for i in range(nq) for j in range(i + 1)]
    i_tab = jnp.asarray(np.array([p[0] for p in pairs], np.int32))
    j_tab = jnp.asarray(np.array([p[1] for p in pairs], np.int32))
    ckv3 = ckv.reshape(batch, seq, KV_LORA)
    kpe3 = kpe.reshape(batch, seq, QK_ROPE)
    return pl.pallas_call(
        _attn_prompt_kernel,
        out_shape=jax.ShapeDtypeStruct(q_lat.shape, BF16),
        grid_spec=pltpu.PrefetchScalarGridSpec(
            num_scalar_prefetch=2,
            grid=(batch, len(pairs)),
            in_specs=[pl.BlockSpec((1, MLA_HEADS, tq, KV_LORA), lambda b, p, it, jt: (b * nq + it[p], 0, 0, 0)),
                      pl.BlockSpec((1, MLA_HEADS, tq, QK_ROPE), lambda b, p, it, jt: (b * nq + it[p], 0, 0, 0)),
                      pl.BlockSpec((1, tq, KV_LORA), lambda b, p, it, jt: (b, jt[p], 0)),
                      pl.BlockSpec((1, tq, QK_ROPE), lambda b, p, it, jt: (b, jt[p], 0))],
            out_specs=pl.BlockSpec((1, MLA_HEADS, tq, KV_LORA), lambda b, p, it, jt: (b * nq + it[p], 0, 0, 0)),
            scratch_shapes=[pltpu.VMEM((tq, KV_LORA), BF16), pltpu.VMEM((tq, QK_ROPE), BF16),
                            pltpu.VMEM((MLA_HEADS, tq, 128), F32), pltpu.VMEM((MLA_HEADS, tq, 128), F32),
                            pltpu.VMEM((MLA_HEADS, tq, KV_LORA), F32)]),
        compiler_params=_cparams(("parallel", "arbitrary"), 48),
        name="mla_prompt_attention",
    )(i_tab, j_tab, q_lat, q_pe, ckv3, kpe3)


def _attn_sample_kernel(*refs, n_pages_step, dec_seq):
    g = n_pages_step
    pt_ref, q_ref, qpe_ref, cn_ref, pn_ref = refs[:5]
    c_refs = refs[5:5 + g]
    p_refs = refs[5 + g:5 + 2 * g]
    o_ref, m_scr, l_scr, acc_scr = refs[5 + 2 * g:]
    del pt_ref
    s_idx = pl.program_id(1)
    rows = MLA_HEADS * dec_seq
    q = q_ref[...].reshape(rows, KV_LORA).astype(BF16)
    qp = qpe_ref[...].reshape(rows, QK_ROPE).astype(BF16)

    def update(s, kc, mask):
        s = s * MLA_SCALE
        if mask is not None:
            s = jnp.where(mask, s, -jnp.inf)
        m_prev = m_scr[...]
        m_new = jnp.maximum(m_prev, jnp.max(s, axis=1, keepdims=True))
        alpha = jnp.exp(m_prev - m_new)
        pr = jnp.exp(s - pltpu.repeat(m_new, s.shape[1] // 128, axis=1))
        l_scr[...] = alpha * l_scr[...] + jnp.sum(pr, axis=1, keepdims=True)
        m_scr[...] = m_new
        acc_scr[...] = acc_scr[...] * pltpu.repeat(alpha, KV_LORA // 128, axis=1) + _dot(pr.astype(BF16), kc)

    @pl.when(s_idx == 0)
    def _():
        m_scr[...] = jnp.full(m_scr.shape, -jnp.inf, F32)
        l_scr[...] = jnp.zeros(l_scr.shape, F32)
        acc_scr[...] = jnp.zeros(acc_scr.shape, F32)
        kc = _pad_rows(cn_ref[0], 128).astype(BF16)
        kp = _pad_rows(pn_ref[0], 128).astype(BF16)
        tpos = lax.broadcasted_iota(jnp.int32, (rows, 128), 0) & (dec_seq - 1)
        update(_dot_nt(q, kc) + _dot_nt(qp, kp), kc, lax.broadcasted_iota(jnp.int32, (rows, 128), 1) <= tpos)

    kc = jnp.concatenate([r[0] for r in c_refs], axis=0).astype(BF16)
    kpt = jnp.concatenate([r[0] for r in p_refs], axis=1).astype(BF16)
    update(_dot_nt(q, kc) + _dot(qp, kpt), kc, None)

    @pl.when(s_idx == pl.num_programs(1) - 1)
    def _():
        out = acc_scr[...] * pltpu.repeat(1.0 / l_scr[...], KV_LORA // 128, axis=1)
        o_ref[...] = out.reshape(o_ref.shape).astype(o_ref.dtype)


def _attn_sample(q_lat, q_pe, ckv_new, kpe_new, cache_ckv, cache_kpe, page_table, dec_batch, dec_seq,
                 n_pages_step=32):
    tq = ATT_TILE
    nt = q_lat.shape[0]
    bpt = tq // dec_seq
    n_pages = page_table.shape[1]
    g = n_pages_step
    q5 = q_lat.reshape(nt, MLA_HEADS, bpt, dec_seq, KV_LORA)
    qp5 = q_pe.reshape(nt, MLA_HEADS, bpt, dec_seq, QK_ROPE)
    cn = ckv_new.reshape(dec_batch, dec_seq, KV_LORA)
    pn = kpe_new.reshape(dec_batch, dec_seq, QK_ROPE)
    q_spec = lambda w: pl.BlockSpec((1, MLA_HEADS, 1, dec_seq, w), lambda b, s, pt: (b // bpt, 0, b % bpt, 0, 0))
    page_spec = lambda shape, gi: pl.BlockSpec((1,) + shape,
                                               lambda b, s, pt: (pt[b * n_pages + s * g + gi], 0, 0))
    in_specs = ([q_spec(KV_LORA), q_spec(QK_ROPE),
                 pl.BlockSpec((1, dec_seq, KV_LORA), lambda b, s, pt: (b, 0, 0)),
                 pl.BlockSpec((1, dec_seq, QK_ROPE), lambda b, s, pt: (b, 0, 0))]
                + [page_spec((PAGE_SIZE, KV_LORA), gi) for gi in range(g)]
                + [page_spec((QK_ROPE, PAGE_SIZE), gi) for gi in range(g)])
    cache_kpe_t = jnp.swapaxes(cache_kpe, 1, 2)
    rows = MLA_HEADS * dec_seq
    out = pl.pallas_call(
        functools.partial(_attn_sample_kernel, n_pages_step=g, dec_seq=dec_seq),
        out_shape=jax.ShapeDtypeStruct(q5.shape, F32),
        grid_spec=pltpu.PrefetchScalarGridSpec(
            num_scalar_prefetch=1,
            grid=(dec_batch, n_pages // g),
            in_specs=in_specs,
            out_specs=q_spec(KV_LORA),
            scratch_shapes=[pltpu.VMEM((rows, 128), F32), pltpu.VMEM((rows, 128), F32),
                            pltpu.VMEM((rows, KV_LORA), F32)]),
        compiler_params=_cparams(("parallel", "arbitrary"), 48),
        name="mla_sample_attention",
    )(page_table.reshape(-1), q5, qp5, cn, pn, *([cache_ckv] * g), *([cache_kpe_t] * g))
    return out.reshape(q_lat.shape)


def _attn_out_kernel(x_ref, o_ref_in, wuv_ref, wo_ref, y_ref, cat_scr):
    for h in range(MLA_HEADS):
        cat_scr[:, h * V_HEAD:(h + 1) * V_HEAD] = _dot(o_ref_in[0, h].astype(BF16), wuv_ref[h])
    y_ref[...] = x_ref[...] + _dot(cat_scr[...].astype(BF16), wo_ref[...])


def _attn_out(x, o_lat, wuv, wo):
    m, d = x.shape
    tm = ATT_TILE
    return pl.pallas_call(
        _attn_out_kernel,
        out_shape=jax.ShapeDtypeStruct((m, d), F32),
        grid=(m // tm,),
        in_specs=[pl.BlockSpec((tm, d), lambda i: (i, 0)),
                  pl.BlockSpec((1, MLA_HEADS, tm, KV_LORA), lambda i: (i, 0, 0, 0)),
                  pl.BlockSpec(wuv.shape, lambda i: (0, 0, 0)),
                  pl.BlockSpec(wo.shape, lambda i: (0, 0))],
        out_specs=pl.BlockSpec((tm, d), lambda i: (i, 0)),
        scratch_shapes=[pltpu.VMEM((tm, MLA_HEADS * V_HEAD), F32)],
        compiler_params=_cparams(("parallel",), 48),
        name="mla_out_proj",
    )(x, o_lat, wuv, wo)


def _moe_kernel(x_ref, g_ref, wr_ref, wg_ref, wu_ref, wd_ref, gf_ref, o_ref, h_scr, gate_scr, acc_scr):
    e = pl.program_id(1)
    f = pl.program_id(2)
    tm = x_ref.shape[0]
    lane = lax.broadcasted_iota(jnp.int32, (tm, 128), 1).astype(F32)

    @pl.when((e == 0) & (f == 0))
    def _():
        x = x_ref[...]
        hn = _rms(x, g_ref[...])
        h_scr[...] = hn.astype(BF16)
        acc_scr[...] = x
        logits = jnp.dot(hn, wr_ref[...], preferred_element_type=F32, precision=lax.Precision.HIGHEST)
        logits = jnp.where(lane < N_EXPERTS, logits, -jnp.inf)
        m1 = jnp.max(logits, axis=1, keepdims=True)
        i1 = jnp.min(jnp.where(logits == m1, lane, 128.0), axis=1, keepdims=True)
        rest = jnp.where(lane == i1, -jnp.inf, logits)
        m2 = jnp.max(rest, axis=1, keepdims=True)
        i2 = jnp.min(jnp.where(rest == m2, lane, 128.0), axis=1, keepdims=True)
        e2 = jnp.exp(m2 - m1)
        w1 = 1.0 / (1.0 + e2)
        gate_scr[...] = jnp.where(lane == i1, w1, 0.0) + jnp.where(lane == i2, e2 * w1, 0.0)

    h = h_scr[...]
    gcol = jnp.sum(jnp.where(lane == e.astype(F32), gate_scr[...], 0.0), axis=1, keepdims=True)
    act = _silu(_dot(h, wg_ref[0])) * _dot(h, wu_ref[0]) * gcol
    acc_scr[...] += _dot(act.astype(BF16), wd_ref[0])

    @pl.when((e == pl.num_programs(1) - 1) & (f == pl.num_programs(2) - 1))
    def _():
        o_ref[...] = _rms(acc_scr[...], gf_ref[...])


def _moe(x, g, w_router_pad, wg, wu, wd, g_final, tm, tf):
    m, d = x.shape
    ne, _, f = wg.shape
    return pl.pallas_call(
        _moe_kernel,
        out_shape=jax.ShapeDtypeStruct((m, d), F32),
        grid=(m // tm, ne, f // tf),
        in_specs=[pl.BlockSpec((tm, d), lambda i, e, j: (i, 0)),
                  pl.BlockSpec((1, d), lambda i, e, j: (0, 0)),
                  pl.BlockSpec((d, 128), lambda i, e, j: (0, 0)),
                  pl.BlockSpec((1, d, tf), lambda i, e, j: (e, 0, j)),
                  pl.BlockSpec((1, d, tf), lambda i, e, j: (e, 0, j)),
                  pl.BlockSpec((1, tf, d), lambda i, e, j: (e, j, 0)),
                  pl.BlockSpec((1, d), lambda i, e, j: (0, 0))],
        out_specs=pl.BlockSpec((tm, d), lambda i, e, j: (i, 0)),
        scratch_shapes=[pltpu.VMEM((tm, d), BF16), pltpu.VMEM((tm, 128), F32), pltpu.VMEM((tm, d), F32)],
        compiler_params=_cparams(("parallel", "arbitrary", "arbitrary"), 56),
        name="moe_experts",
    )(x, g, w_router_pad, wg, wu, wd, g_final)


def _rope_tables(pos, dim, signed):
    half = dim // 2
    inv = ROPE_THETA ** (-jnp.arange(half, dtype=F32) / half)
    ang = pos.astype(F32)[:, None] * inv[None, :]
    c, s = jnp.cos(ang), jnp.sin(ang)
    return jnp.concatenate([c, c], axis=-1), jnp.concatenate([-s if signed else s, s], axis=-1)


def _rot_cols(w):
    half = w.shape[-1] // 2
    return jnp.concatenate([-w[..., half:], w[..., :half]], axis=-1)


def _prep_weights(w_in0, w_gla_gate, b_gla_gate, g_ret_head, g_gla_head, w_out0, w_ffn_gate, w_ffn_up,
                  w_ffn_down, w_dq, w_uq, w_dkv, w_uk, w_uv, w_o_mla, w_router, w_exp_gate, w_exp_up,
                  w_exp_down):
    p = {}
    p["w_in"] = jnp.pad(w_in0, ((0, 0), (0, IN_W_PAD - IN_W))).astype(BF16)
    wgate = w_gla_gate.reshape(GLA_GATE_RANK, N_HEADS_MIX, DK).transpose(1, 0, 2)
    p["wgate"] = jnp.pad(wgate, ((0, 0), (0, 128 - GLA_GATE_RANK), (0, 0))).astype(BF16)
    p["bgate"] = b_gla_gate.reshape(N_HEADS_MIX, 1, DK)
    p["g_ret"] = g_ret_head.reshape(N_HEADS_MIX, 1, DV)
    p["g_gla"] = g_gla_head.reshape(N_HEADS_MIX, 1, DV)
    p["w_out"] = w_out0.astype(BF16)
    p["w_ffn_gate"] = w_ffn_gate.astype(BF16)
    p["w_ffn_up"] = w_ffn_up.astype(BF16)
    p["w_ffn_down"] = w_ffn_down.astype(BF16)
    p["w_dq"] = w_dq.astype(BF16)
    wuq = w_uq.reshape(Q_LORA, MLA_HEADS, QK_NOPE + QK_ROPE).transpose(1, 0, 2)
    p["w_uq_nope"] = wuq[..., :QK_NOPE].astype(BF16)
    p["w_uq_pe"] = wuq[..., QK_NOPE:].astype(BF16)
    p["w_uq_pe_rot"] = _rot_cols(wuq[..., QK_NOPE:]).astype(BF16)
    p["w_uk_t"] = w_uk.transpose(1, 2, 0).astype(BF16)
    p["w_dkv_c"] = w_dkv[:, :KV_LORA].astype(BF16)
    p["w_dkv_pe"] = w_dkv[:, KV_LORA:].astype(BF16)
    p["w_dkv_pe_rot"] = _rot_cols(w_dkv[:, KV_LORA:]).astype(BF16)
    p["w_uv"] = w_uv.transpose(1, 0, 2).astype(BF16)
    p["w_o"] = w_o_mla.astype(BF16)
    p["w_router"] = jnp.pad(w_router, ((0, 0), (0, 128 - N_EXPERTS)))
    p["w_exp_gate"] = w_exp_gate.astype(BF16)
    p["w_exp_up"] = w_exp_up.astype(BF16)
    p["w_exp_down"] = w_exp_down.astype(BF16)
    return p


def _row(v):
    return v.reshape(1, -1)


def _trunk(x, pos, p, norms, *, batch, seq, chunk, s0_ret, s0_gla, attn_fn, act_dtype):
    m = batch * seq
    tm = min(m, 1024)
    cos_r, sin_r = _rope_tables(pos, DK, signed=True)
    proj = _norm_matmul(x, _row(norms["g_norm_mix0"]), p["w_in"], act_dtype, tm, IN_W_PAD // 7)
    mix, ret_s, gla_s = _mixer(proj.reshape(batch, seq, IN_W_PAD), cos_r, sin_r, p["wgate"], p["bgate"],
                               p["g_ret"], p["g_gla"], s0_ret, s0_gla, chunk=chunk, out_dtype=act_dtype)
    x = _resid_matmul(x, mix.reshape(m, 2 * V_W), p["w_out"], min(m, 512))
    x = _ffn(x, _row(norms["g_norm_ffn0"]), p["w_ffn_gate"], p["w_ffn_up"], p["w_ffn_down"], tm, 256)
    cos_m, sin_m = _rope_tables(pos, QK_ROPE, signed=False)
    reps = max(ATT_TILE // seq, 1)
    cos_m, sin_m = jnp.tile(cos_m, (reps, 1)), jnp.tile(sin_m, (reps, 1))
    q_lat, q_pe, ckv, kpe = _mla_prep(x, _row(norms["g_norm_mix1"]), p["w_dq"], _row(norms["g_q"]),
                                      p["w_uq_nope"], p["w_uq_pe"], p["w_uq_pe_rot"], p["w_uk_t"],
                                      p["w_dkv_c"], p["w_dkv_pe"], p["w_dkv_pe_rot"], _row(norms["g_kv"]),
                                      cos_m, sin_m, act_dtype)
    o_lat = attn_fn(q_lat, q_pe, ckv, kpe)
    x = _attn_out(x, o_lat, p["w_uv"], p["w_o"])
    y = _moe(x, _row(norms["g_norm_ffn1"]), p["w_router"], p["w_exp_gate"], p["w_exp_up"], p["w_exp_down"],
             _row(norms["g_final"]), tm, 512)
    return y, ret_s, gla_s, ckv, kpe


def kernel(x_prompt, x_sample, state_ret, state_gla, cache_ckv, cache_kpe, page_table, g_norm_mix0, w_in0, w_gla_gate, b_gla_gate, g_ret_head, g_gla_head, w_out0, g_norm_ffn0, w_ffn_gate, w_ffn_up, w_ffn_down, g_norm_mix1, w_dq, g_q, w_uq, w_dkv, g_kv, w_uk, w_uv, w_o_mla, g_norm_ffn1, w_router, w_exp_gate, w_exp_up, w_exp_down, g_final):
    batch, seq, _ = x_prompt.shape
    dec_batch, dec_seq, _ = x_sample.shape
    p = _prep_weights(w_in0, w_gla_gate, b_gla_gate, g_ret_head, g_gla_head, w_out0, w_ffn_gate, w_ffn_up,
                      w_ffn_down, w_dq, w_uq, w_dkv, w_uk, w_uv, w_o_mla, w_router, w_exp_gate, w_exp_up,
                      w_exp_down)
    norms = dict(g_norm_mix0=g_norm_mix0, g_norm_ffn0=g_norm_ffn0, g_norm_mix1=g_norm_mix1, g_q=g_q,
                 g_kv=g_kv, g_norm_ffn1=g_norm_ffn1, g_final=g_final)

    yp, ret_p, gla_p, ckv_p, kpe_p = _trunk(
        x_prompt.reshape(batch * seq, D_MODEL), jnp.arange(seq), p, norms,
        batch=batch, seq=seq, chunk=128, s0_ret=None, s0_gla=None,
        attn_fn=lambda ql, qp, c, k: _attn_prompt(ql, qp, c, k, batch, seq), act_dtype=BF16)

    ys, ret_s, gla_s, ckv_s, kpe_s = _trunk(
        x_sample.reshape(dec_batch * dec_seq, D_MODEL), PAST_LEN + jnp.arange(dec_seq), p, norms,
        batch=dec_batch, seq=dec_seq, chunk=dec_seq, s0_ret=state_ret, s0_gla=state_gla,
        attn_fn=lambda ql, qp, c, k: _attn_sample(ql, qp, c, k, cache_ckv, cache_kpe, page_table,
                                                  dec_batch, dec_seq),
        act_dtype=F32)

    return (yp.reshape(batch, seq, D_MODEL), ys.reshape(dec_batch, dec_seq, D_MODEL),
            ret_p, gla_p, ckv_p.reshape(batch, seq, KV_LORA), kpe_p.reshape(batch, seq, QK_ROPE),
            ret_s, gla_s, ckv_s.reshape(dec_batch, dec_seq, KV_LORA), kpe_s.reshape(dec_batch, dec_seq, QK_ROPE))
```

```python
import functools
import math

import numpy as np
import jax
import jax.numpy as jnp
from jax import lax
from jax.experimental import pallas as pl
from jax.experimental.pallas import tpu as pltpu

F32 = jnp.float32
BF16 = jnp.bfloat16

D_MODEL = 1024
PAST_LEN = 8192
PAGE_SIZE = 128
N_HEADS_MIX = 4
DK = 128
DV = 256
GLA_GATE_RANK = 16
GLA_TAU = 16.0
QK_W = N_HEADS_MIX * DK
V_W = N_HEADS_MIX * DV
IN_W = 4 * QK_W + 4 * V_W + GLA_GATE_RANK
IN_W_PAD = 4 * QK_W + 4 * V_W + 128
D_FF = 2816
MLA_HEADS = 16
QK_NOPE = 64
QK_ROPE = 32
V_HEAD = 64
Q_LORA = 384
KV_LORA = 256
MLA_SCALE = (QK_NOPE + QK_ROPE) ** -0.5
N_EXPERTS = 8
D_FF_EXPERT = 3584
ROPE_THETA = 10000.0
EPS = 1e-6

OFF_RQ, OFF_RK, OFF_RV, OFF_RG = 0, QK_W, 2 * QK_W, 2 * QK_W + V_W
OFF_GQ = 2 * QK_W + 2 * V_W
OFF_GK = OFF_GQ + QK_W
OFF_GV = OFF_GK + QK_W
OFF_GR = OFF_GV + V_W
OFF_GA = OFF_GR + V_W

LOG_GAMMA = tuple(math.log1p(-2.0 ** (-5.0 - h)) for h in range(N_HEADS_MIX))
GLA_SAFE_LOG_DECAY = -80.0

ATT_TILE = 256
ATT_HEAD_GROUP = 4


def _cparams(sem, vmem_mb):
    return pltpu.CompilerParams(dimension_semantics=sem, vmem_limit_bytes=vmem_mb << 20)


def _rms(x, g):
    return x * lax.rsqrt(jnp.mean(x * x, axis=-1, keepdims=True) + EPS) * g


def _dot(a, b):
    return jnp.dot(a, b, preferred_element_type=F32)


def _dot_nt(a, b):
    return lax.dot_general(a, b, (((1,), (1,)), ((), ())), preferred_element_type=F32)


def _dot_tn(a, b):
    return lax.dot_general(a, b, (((0,), (0,)), ((), ())), preferred_element_type=F32)


def _silu(x):
    return x / (1.0 + jnp.exp(-x))


def _pad_rows(x, n):
    if x.shape[0] == n:
        return x
    return jnp.concatenate([x, jnp.zeros((n - x.shape[0], x.shape[1]), x.dtype)], axis=0)


def _norm_matmul_kernel(x_ref, g_ref, w_ref, o_ref, h_scr):
    @pl.when(pl.program_id(1) == 0)
    def _():
        h_scr[...] = _rms(x_ref[...], g_ref[...]).astype(BF16)

    o_ref[...] = _dot(h_scr[...], w_ref[...]).astype(o_ref.dtype)


def _norm_matmul(x, g, w, out_dtype, tm, tn):
    m, d = x.shape
    n = w.shape[1]
    return pl.pallas_call(
        _norm_matmul_kernel,
        out_shape=jax.ShapeDtypeStruct((m, n), out_dtype),
        grid=(m // tm, n // tn),
        in_specs=[pl.BlockSpec((tm, d), lambda i, j: (i, 0)),
                  pl.BlockSpec((1, d), lambda i, j: (0, 0)),
                  pl.BlockSpec((d, tn), lambda i, j: (0, j))],
        out_specs=pl.BlockSpec((tm, tn), lambda i, j: (i, j)),
        scratch_shapes=[pltpu.VMEM((tm, d), BF16)],
        compiler_params=_cparams(("parallel", "arbitrary"), 48),
        name="norm_in_proj",
    )(x, g, w)


def _cumsum_rows(x, n):
    row = lax.broadcasted_iota(jnp.int32, x.shape, 0)
    s = 1
    while s < n:
        x = x + jnp.where(row >= s, pltpu.roll(x, s, 0), 0.0)
        s *= 2
    return x


def _mixer_kernel(*refs, cq, ck, has_state, mm_dtype):
    if has_state:
        (proj, cos, sin, wgate, bgate, g_ret, g_gla, s0r, s0g,
         o_ref, ro_ref, go_ref, sr, sg, kf_scr, bc_scr, vf_scr) = refs
    else:
        (proj, cos, sin, wgate, bgate, g_ret, g_gla,
         o_ref, ro_ref, go_ref, sr, sg, kf_scr, bc_scr, vf_scr) = refs
    t = pl.program_id(1)

    @pl.when(t == 0)
    def _init():
        for h in range(N_HEADS_MIX):
            if has_state:
                sr[h] = s0r[0, h]
                sg[h] = s0g[0, h].T
            else:
                sr[h] = jnp.zeros((DK, DV), F32)
                sg[h] = jnp.zeros((DV, DK), F32)

    row = lax.broadcasted_iota(jnp.int32, (cq, ck), 0)
    col = lax.broadcasted_iota(jnp.int32, (cq, ck), 1)
    causal = row >= col
    relf = jnp.maximum(row - col, 0).astype(F32)
    ridx = lax.broadcasted_iota(jnp.int32, (cq, 1), 0).astype(F32)
    cosv = cos[...]
    sinv = sin[...]

    def rope(x):
        return x * cosv + pltpu.roll(x, DK // 2, 1) * sinv

    def col_slice(off, h, w):
        return proj[0, :, off + h * w: off + (h + 1) * w]

    for h in range(N_HEADS_MIX):
        lg = LOG_GAMMA[h]
        dmask = jnp.where(causal, jnp.exp(lg * relf), 0.0)
        qdec = jnp.exp(lg * (ridx + 1.0))
        kdec = jnp.exp(lg * (cq - 1.0 - ridx))
        cdec = math.exp(lg * cq)
        q = rope(col_slice(OFF_RQ, h, DK).astype(F32)).astype(mm_dtype)
        k = rope(col_slice(OFF_RK, h, DK).astype(F32)) * (DK ** -0.5)
        v = _pad_rows(col_slice(OFF_RV, h, DV).astype(mm_dtype), ck)
        a = _dot_nt(q, _pad_rows(k.astype(mm_dtype), ck)) * dmask
        s_old = sr[h]
        o = _dot(a.astype(mm_dtype), v) + _dot(q, s_old.astype(mm_dtype)) * qdec
        sr[h] = s_old * cdec + _dot_tn(_pad_rows((k * kdec).astype(mm_dtype), ck), v)
        oc = o - jnp.mean(o, axis=-1, keepdims=True)
        on = oc * lax.rsqrt(jnp.mean(oc * oc, axis=-1, keepdims=True) + EPS) * g_ret[h]
        gate = col_slice(OFF_RG, h, DV).astype(F32)
        o_ref[0, :, h * DV:(h + 1) * DV] = (on * _silu(gate)).astype(o_ref.dtype)

        ga = proj[0, :, OFF_GA:OFF_GA + 128].astype(BF16)
        z = _dot(ga, wgate[h]) + bgate[h]
        g = (jnp.minimum(z, 0.0) - jnp.log1p(jnp.exp(-jnp.abs(z)))) * (1.0 / GLA_TAU)
        bc = _cumsum_rows(g, cq)
        blast = bc[cq - 1:cq, :]
        qs = col_slice(OFF_GQ, h, DK).astype(F32) * (DK ** -0.5)
        kf = col_slice(OFF_GK, h, DK).astype(F32)
        vraw = col_slice(OFF_GV, h, DV)
        qe = (qs * jnp.exp(bc)).astype(mm_dtype)
        vp = _pad_rows(vraw.astype(mm_dtype), ck)

        def intra_fast():
            kinv = _pad_rows((kf * jnp.exp(-bc)).astype(mm_dtype), ck)
            af = jnp.where(causal, _dot_nt(qe, kinv), 0.0)
            return _dot(af.astype(mm_dtype), vp)

        def intra_exact():
            kf_scr[...] = kf
            bc_scr[...] = bc
            vf_scr[...] = vraw.astype(F32)

            def body(j, acc):
                kj = kf_scr[pl.ds(j, 1), :]
                bj = bc_scr[pl.ds(j, 1), :]
                vj = vf_scr[pl.ds(j, 1), :]
                w = jnp.exp(jnp.minimum(bc - bj, 0.0))
                c = jnp.sum(qs * kj * w, axis=1, keepdims=True)
                c = jnp.where(ridx >= lax.convert_element_type(j, F32), c, 0.0)
                return acc + c * vj

            return lax.fori_loop(0, cq, body, jnp.zeros((cq, DV), F32))

        o_intra = lax.cond(jnp.min(blast) >= GLA_SAFE_LOG_DECAY, intra_fast, intra_exact)
        st_old = sg[h]
        og = o_intra + _dot_nt(qe, st_old.astype(mm_dtype))
        kk = _pad_rows((kf * jnp.exp(blast - bc)).astype(mm_dtype), ck)
        sg[h] = st_old * jnp.exp(blast) + _dot_tn(vp, kk)
        ogn = og * lax.rsqrt(jnp.mean(og * og, axis=-1, keepdims=True) + EPS) * g_gla[h]
        gateg = col_slice(OFF_GR, h, DV).astype(F32)
        o_ref[0, :, V_W + h * DV:V_W + (h + 1) * DV] = (ogn * _silu(gateg)).astype(o_ref.dtype)

    @pl.when(t == pl.num_programs(1) - 1)
    def _fin():
        for h in range(N_HEADS_MIX):
            ro_ref[0, h] = sr[h]
            go_ref[0, h] = sg[h].T


def _mixer(proj, cos, sin, wgate, bgate, g_ret, g_gla, s0_ret, s0_gla, *, chunk, out_dtype):
    b, t, w = proj.shape
    cq = chunk
    ck = max(chunk, 128)
    has_state = s0_ret is not None
    mm_dtype = BF16 if chunk >= 128 else F32
    const = lambda *shape: pl.BlockSpec(shape, lambda i, j: (0,) * len(shape))
    state_spec = pl.BlockSpec((1, N_HEADS_MIX, DK, DV), lambda i, j: (i, 0, 0, 0))
    in_specs = [pl.BlockSpec((1, cq, w), lambda i, j: (i, j, 0)),
                pl.BlockSpec((cq, DK), lambda i, j: (j, 0)),
                pl.BlockSpec((cq, DK), lambda i, j: (j, 0)),
                const(N_HEADS_MIX, 128, DK), const(N_HEADS_MIX, 1, DK),
                const(N_HEADS_MIX, 1, DV), const(N_HEADS_MIX, 1, DV)]
    args = [proj, cos, sin, wgate, bgate, g_ret, g_gla]
    if has_state:
        in_specs += [state_spec, state_spec]
        args += [s0_ret, s0_gla]
    return pl.pallas_call(
        functools.partial(_mixer_kernel, cq=cq, ck=ck, has_state=has_state, mm_dtype=mm_dtype),
        out_shape=(jax.ShapeDtypeStruct((b, t, 2 * V_W), out_dtype),
                   jax.ShapeDtypeStruct((b, N_HEADS_MIX, DK, DV), F32),
                   jax.ShapeDtypeStruct((b, N_HEADS_MIX, DK, DV), F32)),
        grid=(b, t // cq),
        in_specs=in_specs,
        out_specs=(pl.BlockSpec((1, cq, 2 * V_W), lambda i, j: (i, j, 0)), state_spec, state_spec),
        scratch_shapes=[pltpu.VMEM((N_HEADS_MIX, DK, DV), F32), pltpu.VMEM((N_HEADS_MIX, DV, DK), F32),
                        pltpu.VMEM((cq, DK), F32), pltpu.VMEM((cq, DK), F32), pltpu.VMEM((cq, DV), F32)],
        compiler_params=_cparams(("parallel", "arbitrary"), 48),
        name="retention_gla_mixer",
    )(*args)


def _resid_matmul_kernel(x_ref, a_ref, w_ref, o_ref):
    o_ref[...] = x_ref[...] + _dot(a_ref[...].astype(BF16), w_ref[...])


def _resid_matmul(x, a, w, tm):
    m, d = x.shape
    k = a.shape[1]
    return pl.pallas_call(
        _resid_matmul_kernel,
        out_shape=jax.ShapeDtypeStruct((m, d), F32),
        grid=(m // tm,),
        in_specs=[pl.BlockSpec((tm, d), lambda i: (i, 0)),
                  pl.BlockSpec((tm, k), lambda i: (i, 0)),
                  pl.BlockSpec((k, d), lambda i: (0, 0))],
        out_specs=pl.BlockSpec((tm, d), lambda i: (i, 0)),
        compiler_params=_cparams(("parallel",), 48),
        name="resid_out_proj",
    )(x, a, w)


def _ffn_kernel(x_ref, g_ref, wg_ref, wu_ref, wd_ref, o_ref, h_scr, acc_scr):
    f = pl.program_id(1)

    @pl.when(f == 0)
    def _():
        x = x_ref[...]
        h_scr[...] = _rms(x, g_ref[...]).astype(BF16)
        acc_scr[...] = x

    h = h_scr[...]
    act = _silu(_dot(h, wg_ref[...])) * _dot(h, wu_ref[...])
    acc_scr[...] += _dot(act.astype(BF16), wd_ref[...])

    @pl.when(f == pl.num_programs(1) - 1)
    def _():
        o_ref[...] = acc_scr[...]


def _ffn(x, g, wg, wu, wd, tm, tf):
    m, d = x.shape
    f = wg.shape[1]
    return pl.pallas_call(
        _ffn_kernel,
        out_shape=jax.ShapeDtypeStruct((m, d), F32),
        grid=(m // tm, f // tf),
        in_specs=[pl.BlockSpec((tm, d), lambda i, j: (i, 0)),
                  pl.BlockSpec((1, d), lambda i, j: (0, 0)),
                  pl.BlockSpec((d, tf), lambda i, j: (0, j)),
                  pl.BlockSpec((d, tf), lambda i, j: (0, j)),
                  pl.BlockSpec((tf, d), lambda i, j: (j, 0))],
        out_specs=pl.BlockSpec((tm, d), lambda i, j: (i, 0)),
        scratch_shapes=[pltpu.VMEM((tm, d), BF16), pltpu.VMEM((tm, d), F32)],
        compiler_params=_cparams(("parallel", "arbitrary"), 48),
        name="swiglu_ffn",
    )(x, g, wg, wu, wd)


def _mla_prep_kernel(x_ref, g1_ref, wdq_ref, gq_ref, wn_ref, wp_ref, wpr_ref, wuk_ref,
                     wkc_ref, wkp_ref, wkpr_ref, gkv_ref, cos_ref, sin_ref,
                     qlat_ref, qpe_ref, ckv_ref, kpe_ref):
    hn = _rms(x_ref[...], g1_ref[...]).astype(BF16)
    cq = _rms(_dot(hn, wdq_ref[...]), gq_ref[...]).astype(BF16)
    cosv = cos_ref[...]
    sinv = sin_ref[...]
    ckv_ref[...] = _rms(_dot(hn, wkc_ref[...]), gkv_ref[...])
    kpe_ref[...] = _dot(hn, wkp_ref[...]) * cosv + _dot(hn, wkpr_ref[...]) * sinv
    for h in range(MLA_HEADS):
        qn = _dot(cq, wn_ref[h]).astype(BF16)
        qlat_ref[0, h] = _dot(qn, wuk_ref[h]).astype(qlat_ref.dtype)
        qpe_ref[0, h] = (_dot(cq, wp_ref[h]) * cosv + _dot(cq, wpr_ref[h]) * sinv).astype(qpe_ref.dtype)


def _mla_prep(x, g1, wdq, gq, wn, wp, wpr, wuk, wkc, wkp, wkpr, gkv, cos, sin, q_dtype):
    m, d = x.shape
    tm = ATT_TILE
    nt = m // tm
    n_pos_tiles = cos.shape[0] // tm
    full = lambda a: pl.BlockSpec(a.shape, lambda i: (0,) * a.ndim)
    return pl.pallas_call(
        _mla_prep_kernel,
        out_shape=(jax.ShapeDtypeStruct((nt, MLA_HEADS, tm, KV_LORA), q_dtype),
                   jax.ShapeDtypeStruct((nt, MLA_HEADS, tm, QK_ROPE), q_dtype),
                   jax.ShapeDtypeStruct((m, KV_LORA), F32),
                   jax.ShapeDtypeStruct((m, QK_ROPE), F32)),
        grid=(nt,),
        in_specs=[pl.BlockSpec((tm, d), lambda i: (i, 0)), full(g1), full(wdq), full(gq), full(wn), full(wp),
                  full(wpr), full(wuk), full(wkc), full(wkp), full(wkpr), full(gkv),
                  pl.BlockSpec((tm, QK_ROPE), lambda i: (i % n_pos_tiles, 0)),
                  pl.BlockSpec((tm, QK_ROPE), lambda i: (i % n_pos_tiles, 0))],
        out_specs=(pl.BlockSpec((1, MLA_HEADS, tm, KV_LORA), lambda i: (i, 0, 0, 0)),
                   pl.BlockSpec((1, MLA_HEADS, tm, QK_ROPE), lambda i: (i, 0, 0, 0)),
                   pl.BlockSpec((tm, KV_LORA), lambda i: (i, 0)),
                   pl.BlockSpec((tm, QK_ROPE), lambda i: (i, 0))),
        compiler_params=_cparams(("parallel",), 48),
        name="mla_prep",
    )(x, g1, wdq, gq, wn, wp, wpr, wuk, wkc, wkp, wkpr, gkv, cos, sin)


def _attn_prompt_kernel(it_ref, jt_ref, q_ref, qpe_ref, ckv_ref, kpe_ref, o_ref,
                        kc_scr, kp_scr, m_scr, l_scr, acc_scr):
    p = pl.program_id(1)
    i = it_ref[p]
    j = jt_ref[p]
    tq = ATT_TILE
    hb = ATT_HEAD_GROUP
    rows = hb * tq

    @pl.when(j == 0)
    def _():
        m_scr[...] = jnp.full(m_scr.shape, -jnp.inf, F32)
        l_scr[...] = jnp.zeros(l_scr.shape, F32)
        acc_scr[...] = jnp.zeros(acc_scr.shape, F32)

    kc_scr[...] = ckv_ref[0].astype(BF16)
    kp_scr[...] = kpe_ref[0].astype(BF16)

    def make_body(masked):
        def body(gidx, carry):
            hs = pl.ds(gidx * hb, hb)
            q = q_ref[0, hs].reshape(rows, KV_LORA)
            qp = qpe_ref[0, hs].reshape(rows, QK_ROPE)
            kc = kc_scr[...]
            s = (_dot_nt(q, kc) + _dot_nt(qp, kp_scr[...])) * MLA_SCALE
            if masked:
                qpos = lax.broadcasted_iota(jnp.int32, (rows, tq), 0) & (tq - 1)
                s = jnp.where(lax.broadcasted_iota(jnp.int32, (rows, tq), 1) <= qpos, s, -jnp.inf)
            m_prev = m_scr[hs].reshape(rows, 128)
            m_new = jnp.maximum(m_prev, jnp.max(s, axis=1, keepdims=True))
            alpha = jnp.exp(m_prev - m_new)
            pr = jnp.exp(s - pltpu.repeat(m_new, tq // 128, axis=1))
            l_new = alpha * l_scr[hs].reshape(rows, 128) + jnp.sum(pr, axis=1, keepdims=True)
            l_scr[hs] = l_new.reshape(hb, tq, 128)
            m_scr[hs] = m_new.reshape(hb, tq, 128)
            acc = (acc_scr[hs].reshape(rows, KV_LORA) * pltpu.repeat(alpha, KV_LORA // 128, axis=1)
                   + _dot(pr.astype(BF16), kc))
            acc_scr[hs] = acc.reshape(hb, tq, KV_LORA)
            return carry
        return body

    @pl.when(j < i)
    def _():
        lax.fori_loop(0, MLA_HEADS // hb, make_body(False), 0)

    @pl.when(j == i)
    def _():
        lax.fori_loop(0, MLA_HEADS // hb, make_body(True), 0)
        inv_l = 1.0 / l_scr[...].reshape(MLA_HEADS * tq, 128)
        acc = acc_scr[...].reshape(MLA_HEADS * tq, KV_LORA) * pltpu.repeat(inv_l, KV_LORA // 128, axis=1)
        o_ref[0] = acc.reshape(MLA_HEADS, tq, KV_LORA).astype(o_ref.dtype)


def _attn_prompt(q_lat, q_pe, ckv, kpe, batch, seq):
    tq = ATT_TILE
    nq = seq // tq
    pairs = [(i, j) for i in range(nq) for j in range(i + 1)]
    i_tab = jnp.asarray(np.array([p[0] for p in pairs], np.int32))
    j_tab = jnp.asarray(np.array([p[1] for p in pairs], np.int32))
    ckv3 = ckv.reshape(batch, seq, KV_LORA)
    kpe3 = kpe.reshape(batch, seq, QK_ROPE)
    return pl.pallas_call(
        _attn_prompt_kernel,
        out_shape=jax.ShapeDtypeStruct(q_lat.shape, BF16),
        grid_spec=pltpu.PrefetchScalarGridSpec(
            num_scalar_prefetch=2,
            grid=(batch, len(pairs)),
            in_specs=[pl.BlockSpec((1, MLA_HEADS, tq, KV_LORA), lambda b, p, it, jt: (b * nq + it[p], 0, 0, 0)),
                      pl.BlockSpec((1, MLA_HEADS, tq, QK_ROPE), lambda b, p, it, jt: (b * nq + it[p], 0, 0, 0)),
                      pl.BlockSpec((1, tq, KV_LORA), lambda b, p, it, jt: (b, jt[p], 0)),
                      pl.BlockSpec((1, tq, QK_ROPE), lambda b, p, it, jt: (b, jt[p], 0))],
            out_specs=pl.BlockSpec((1, MLA_HEADS, tq, KV_LORA), lambda b, p, it, jt: (b * nq + it[p], 0, 0, 0)),
            scratch_shapes=[pltpu.VMEM((tq, KV_LORA), BF16), pltpu.VMEM((tq, QK_ROPE), BF16),
                            pltpu.VMEM((MLA_HEADS, tq, 128), F32), pltpu.VMEM((MLA_HEADS, tq, 128), F32),
                            pltpu.VMEM((MLA_HEADS, tq, KV_LORA), F32)]),
        compiler_params=_cparams(("parallel", "arbitrary"), 48),
        name="mla_prompt_attention",
    )(i_tab, j_tab, q_lat, q_pe, ckv3, kpe3)


def _attn_sample_kernel(*refs, n_pages_step, dec_seq):
    g = n_pages_step
    pt_ref, q_ref, qpe_ref, cn_ref, pn_ref = refs[:5]
    c_refs = refs[5:5 + g]
    p_refs = refs[5 + g:5 + 2 * g]
    o_ref, m_scr, l_scr, acc_scr = refs[5 + 2 * g:]
    del pt_ref
    s_idx = pl.program_id(1)
    rows = MLA_HEADS * dec_seq
    q = q_ref[...].reshape(rows, KV_LORA).astype(BF16)
    qp = qpe_ref[...].reshape(rows, QK_ROPE).astype(BF16)

    def update(s, kc, mask):
        s = s * MLA_SCALE
        if mask is not None:
            s = jnp.where(mask, s, -jnp.inf)
        m_prev = m_scr[...]
        m_new = jnp.maximum(m_prev, jnp.max(s, axis=1, keepdims=True))
        alpha = jnp.exp(m_prev - m_new)
        pr = jnp.exp(s - pltpu.repeat(m_new, s.shape[1] // 128, axis=1))
        l_scr[...] = alpha * l_scr[...] + jnp.sum(pr, axis=1, keepdims=True)
        m_scr[...] = m_new
        acc_scr[...] = acc_scr[...] * pltpu.repeat(alpha, KV_LORA // 128, axis=1) + _dot(pr.astype(BF16), kc)

    @pl.when(s_idx == 0)
    def _():
        m_scr[...] = jnp.full(m_scr.shape, -jnp.inf, F32)
        l_scr[...] = jnp.zeros(l_scr.shape, F32)
        acc_scr[...] = jnp.zeros(acc_scr.shape, F32)
        kc = _pad_rows(cn_ref[0], 128).astype(BF16)
        kp = _pad_rows(pn_ref[0], 128).astype(BF16)
        tpos = lax.broadcasted_iota(jnp.int32, (rows, 128), 0) & (dec_seq - 1)
        update(_dot_nt(q, kc) + _dot_nt(qp, kp), kc, lax.broadcasted_iota(jnp.int32, (rows, 128), 1) <= tpos)

    kc = jnp.concatenate([r[0] for r in c_refs], axis=0).astype(BF16)
    kpt = jnp.concatenate([r[0] for r in p_refs], axis=1).astype(BF16)
    update(_dot_nt(q, kc) + _dot(qp, kpt), kc, None)

    @pl.when(s_idx == pl.num_programs(1) - 1)
    def _():
        out = acc_scr[...] * pltpu.repeat(1.0 / l_scr[...], KV_LORA // 128, axis=1)
        o_ref[...] = out.reshape(o_ref.shape).astype(o_ref.dtype)


def _attn_sample(q_lat, q_pe, ckv_new, kpe_new, cache_ckv, cache_kpe, page_table, dec_batch, dec_seq,
                 n_pages_step=32):
    tq = ATT_TILE
    nt = q_lat.shape[0]
    bpt = tq // dec_seq
    n_pages = page_table.shape[1]
    g = n_pages_step
    q5 = q_lat.reshape(nt, MLA_HEADS, bpt, dec_seq, KV_LORA)
    qp5 = q_pe.reshape(nt, MLA_HEADS, bpt, dec_seq, QK_ROPE)
    cn = ckv_new.reshape(dec_batch, dec_seq, KV_LORA)
    pn = kpe_new.reshape(dec_batch, dec_seq, QK_ROPE)
    q_spec = lambda w: pl.BlockSpec((1, MLA_HEADS, 1, dec_seq, w), lambda b, s, pt: (b // bpt, 0, b % bpt, 0, 0))
    page_spec = lambda shape, gi: pl.BlockSpec((1,) + shape,
                                               lambda b, s, pt: (pt[b * n_pages + s * g + gi], 0, 0))
    in_specs = ([q_spec(KV_LORA), q_spec(QK_ROPE),
                 pl.BlockSpec((1, dec_seq, KV_LORA), lambda b, s, pt: (b, 0, 0)),
                 pl.BlockSpec((1, dec_seq, QK_ROPE), lambda b, s, pt: (b, 0, 0))]
                + [page_spec((PAGE_SIZE, KV_LORA), gi) for gi in range(g)]
                + [page_spec((QK_ROPE, PAGE_SIZE), gi) for gi in range(g)])
    cache_kpe_t = jnp.swapaxes(cache_kpe, 1, 2)
    rows = MLA_HEADS * dec_seq
    out = pl.pallas_call(
        functools.partial(_attn_sample_kernel, n_pages_step=g, dec_seq=dec_seq),
        out_shape=jax.ShapeDtypeStruct(q5.shape, F32),
        grid_spec=pltpu.PrefetchScalarGridSpec(
            num_scalar_prefetch=1,
            grid=(dec_batch, n_pages // g),
            in_specs=in_specs,
            out_specs=q_spec(KV_LORA),
            scratch_shapes=[pltpu.VMEM((rows, 128), F32), pltpu.VMEM((rows, 128), F32),
                            pltpu.VMEM((rows, KV_LORA), F32)]),
        compiler_params=_cparams(("parallel", "arbitrary"), 48),
        name="mla_sample_attention",
    )(page_table.reshape(-1), q5, qp5, cn, pn, *([cache_ckv] * g), *([cache_kpe_t] * g))
    return out.reshape(q_lat.shape)


def _attn_out_kernel(x_ref, o_ref_in, wuv_ref, wo_ref, y_ref, cat_scr):
    for h in range(MLA_HEADS):
        cat_scr[:, h * V_HEAD:(h + 1) * V_HEAD] = _dot(o_ref_in[0, h].astype(BF16), wuv_ref[h])
    y_ref[...] = x_ref[...] + _dot(cat_scr[...].astype(BF16), wo_ref[...])


def _attn_out(x, o_lat, wuv, wo):
    m, d = x.shape
    tm = ATT_TILE
    return pl.pallas_call(
        _attn_out_kernel,
        out_shape=jax.ShapeDtypeStruct((m, d), F32),
        grid=(m // tm,),
        in_specs=[pl.BlockSpec((tm, d), lambda i: (i, 0)),
                  pl.BlockSpec((1, MLA_HEADS, tm, KV_LORA), lambda i: (i, 0, 0, 0)),
                  pl.BlockSpec(wuv.shape, lambda i: (0, 0, 0)),
                  pl.BlockSpec(wo.shape, lambda i: (0, 0))],
        out_specs=pl.BlockSpec((tm, d), lambda i: (i, 0)),
        scratch_shapes=[pltpu.VMEM((tm, MLA_HEADS * V_HEAD), F32)],
        compiler_params=_cparams(("parallel",), 48),
        name="mla_out_proj",
    )(x, o_lat, wuv, wo)


MOE_BLOCK = 512


def _lane_col(table, e):
    lane = lax.broadcasted_iota(jnp.int32, table.shape, 1)
    return jnp.sum(jnp.where(lane == e, table, 0.0), axis=1, keepdims=True)


def _router_kernel(x_ref, g_ref, wr_ref, h_ref, gate_ref):
    tm = x_ref.shape[0]
    lane = lax.broadcasted_iota(jnp.int32, (tm, 128), 1).astype(F32)
    hn = _rms(x_ref[...], g_ref[...])
    h_ref[...] = hn.astype(BF16)
    logits = jnp.dot(hn, wr_ref[...], preferred_element_type=F32, precision=lax.Precision.HIGHEST)
    logits = jnp.where(lane < N_EXPERTS, logits, -jnp.inf)
    m1 = jnp.max(logits, axis=1, keepdims=True)
    i1 = jnp.min(jnp.where(logits == m1, lane, 128.0), axis=1, keepdims=True)
    rest = jnp.where(lane == i1, -jnp.inf, logits)
    m2 = jnp.max(rest, axis=1, keepdims=True)
    i2 = jnp.min(jnp.where(rest == m2, lane, 128.0), axis=1, keepdims=True)
    e2 = jnp.exp(m2 - m1)
    w1 = 1.0 / (1.0 + e2)
    gate_ref[...] = jnp.where(lane == i1, w1, 0.0) + jnp.where(lane == i2, e2 * w1, 0.0)


def _router(x, g, w_router_pad):
    m, d = x.shape
    tm = MOE_BLOCK
    return pl.pallas_call(
        _router_kernel,
        out_shape=(jax.ShapeDtypeStruct((m, d), BF16), jax.ShapeDtypeStruct((m, 128), F32)),
        grid=(m // tm,),
        in_specs=[pl.BlockSpec((tm, d), lambda i: (i, 0)),
                  pl.BlockSpec((1, d), lambda i: (0, 0)),
                  pl.BlockSpec((d, 128), lambda i: (0, 0))],
        out_specs=(pl.BlockSpec((tm, d), lambda i: (i, 0)), pl.BlockSpec((tm, 128), lambda i: (i, 0))),
        compiler_params=_cparams(("parallel",), 48),
        name="moe_router",
    )(x, g, w_router_pad)


def _route_tables(gate, n_blocks):
    r = MOE_BLOCK
    m = gate.shape[0]
    nt = m // r
    sel = gate[:, :N_EXPERTS] > 0.0
    csum = jnp.cumsum(sel.astype(jnp.int32), axis=0)
    nblk = (csum[-1] + r - 1) // r
    blk_end = jnp.cumsum(nblk)
    blk_start = blk_end - nblk
    slot = jnp.where(sel, blk_start[None, :] * r + csum - 1, -1)
    block_expert = jnp.minimum(jnp.searchsorted(blk_end, jnp.arange(n_blocks, dtype=jnp.int32), side="right"),
                               N_EXPERTS - 1).astype(jnp.int32)
    r1 = csum[r - 1::r]
    r0 = jnp.concatenate([jnp.zeros((1, N_EXPERTS), jnp.int32), r1[:-1]], axis=0)
    has = r1 > r0
    b0 = blk_start[None, :] + r0 // r
    b1 = blk_start[None, :] + jnp.maximum(r1 - 1, 0) // r
    blk = jnp.stack([b0, jnp.where(has & (b1 > b0), b1, b0)], axis=-1)
    valid = jnp.stack([has, has & (b1 > b0)], axis=-1)
    blk = jnp.clip(blk, 0, n_blocks - 1)
    comb_blk = blk.reshape(-1).astype(jnp.int32)
    comb_valid = valid.reshape(-1).astype(jnp.int32)
    d_valid = valid.transpose(1, 0, 2).reshape(-1)
    seen = lax.cummax(jnp.where(d_valid, blk.transpose(1, 0, 2).reshape(-1), -1), axis=0).astype(jnp.int32)
    prev = jnp.concatenate([jnp.full((1,), -1, jnp.int32), seen[:-1]])
    d_first = (d_valid & (seen != prev)).astype(jnp.int32)
    d_blk = jnp.maximum(seen, 0)
    slot_f = slot.astype(F32)
    slot_cols = jnp.pad(slot_f, ((0, 0), (0, 128 - N_EXPERTS)), constant_values=-1.0)
    return dict(slot_rows=slot_f.T, slot_cols=slot_cols, block_expert=block_expert,
                n_used=blk_end[-1:].astype(jnp.int32), comb_blk=comb_blk, comb_valid=comb_valid,
                d_blk=d_blk, d_valid=d_valid.astype(jnp.int32), d_first=d_first, nt=nt)


def _dispatch_kernel(blk_ref, valid_ref, first_ref, h_ref, slot_ref, o_ref, acc_scr):
    e, i, j = pl.program_id(0), pl.program_id(1), pl.program_id(2)
    w = (e * pl.num_programs(1) + i) * 2 + j
    r = MOE_BLOCK

    @pl.when(valid_ref[w] == 1)
    def _():
        @pl.when(first_ref[w] == 1)
        def _():
            acc_scr[...] = jnp.zeros(acc_scr.shape, F32)

        local = slot_ref[pl.ds(e, 1), :] - lax.convert_element_type(blk_ref[w] * r, F32)
        sel = (lax.broadcasted_iota(jnp.int32, (r, r), 0).astype(F32) == local).astype(BF16)
        acc_scr[...] += _dot(sel, h_ref[...])
        o_ref[...] = acc_scr[...].astype(o_ref.dtype)


def _dispatch(h, tabs, n_blocks):
    m, d = h.shape
    r = MOE_BLOCK
    nt = m // r
    widx = lambda e, i, j: (e * nt + i) * 2 + j
    return pl.pallas_call(
        _dispatch_kernel,
        out_shape=jax.ShapeDtypeStruct((n_blocks * r, d), BF16),
        grid_spec=pltpu.PrefetchScalarGridSpec(
            num_scalar_prefetch=3,
            grid=(N_EXPERTS, nt, 2),
            in_specs=[pl.BlockSpec((r, d), lambda e, i, j, bt, vt, ft: (i, 0)),
                      pl.BlockSpec((N_EXPERTS, r), lambda e, i, j, bt, vt, ft: (0, i))],
            out_specs=pl.BlockSpec((r, d), lambda e, i, j, bt, vt, ft: (bt[widx(e, i, j)], 0)),
            scratch_shapes=[pltpu.VMEM((r, d), F32)]),
        compiler_params=_cparams(("arbitrary", "arbitrary", "arbitrary"), 48),
        name="moe_dispatch",
    )(tabs["d_blk"], tabs["d_valid"], tabs["d_first"], h, tabs["slot_rows"])


def _experts_kernel(be_ref, nu_ref, x_ref, wg_ref, wu_ref, wd_ref, o_ref, acc_scr):
    del be_ref
    k, f = pl.program_id(0), pl.program_id(1)

    @pl.when(k < nu_ref[0])
    def _():
        x = x_ref[...]
        act = _silu(_dot(x, wg_ref[0])) * _dot(x, wu_ref[0])
        part = _dot(act.astype(BF16), wd_ref[0])

        @pl.when(f == 0)
        def _():
            acc_scr[...] = part

        @pl.when(f > 0)
        def _():
            acc_scr[...] += part

        @pl.when(f == pl.num_programs(1) - 1)
        def _():
            o_ref[...] = acc_scr[...].astype(o_ref.dtype)


def _experts(xs, tabs, wg, wu, wd, tf):
    rows, d = xs.shape
    r = MOE_BLOCK
    nb = rows // r
    nf = wg.shape[2] // tf
    kk = lambda k, nu: jnp.minimum(k, nu[0] - 1)
    jj = lambda k, j, nu: jnp.where(k < nu[0], j, nf - 1)
    return pl.pallas_call(
        _experts_kernel,
        out_shape=jax.ShapeDtypeStruct((rows, d), BF16),
        grid_spec=pltpu.PrefetchScalarGridSpec(
            num_scalar_prefetch=2,
            grid=(nb, nf),
            in_specs=[pl.BlockSpec((r, d), lambda k, j, be, nu: (kk(k, nu), 0)),
                      pl.BlockSpec((1, d, tf), lambda k, j, be, nu: (be[kk(k, nu)], 0, jj(k, j, nu))),
                      pl.BlockSpec((1, d, tf), lambda k, j, be, nu: (be[kk(k, nu)], 0, jj(k, j, nu))),
                      pl.BlockSpec((1, tf, d), lambda k, j, be, nu: (be[kk(k, nu)], jj(k, j, nu), 0))],
            out_specs=pl.BlockSpec((r, d), lambda k, j, be, nu: (kk(k, nu), 0)),
            scratch_shapes=[pltpu.VMEM((r, d), F32)]),
        compiler_params=_cparams(("arbitrary", "arbitrary"), 48),
        name="moe_experts",
    )(tabs["block_expert"], tabs["n_used"], xs, wg, wu, wd)


def _combine_kernel(blk_ref, valid_ref, x_ref, gate_ref, slot_ref, y_ref, gf_ref, o_ref, acc_scr):
    i, e, j = pl.program_id(0), pl.program_id(1), pl.program_id(2)
    w = (i * N_EXPERTS + e) * 2 + j
    r = MOE_BLOCK

    @pl.when((e == 0) & (j == 0))
    def _():
        acc_scr[...] = x_ref[...]

    @pl.when(valid_ref[w] == 1)
    def _():
        local = _lane_col(slot_ref[...], e) - lax.convert_element_type(blk_ref[w] * r, F32)
        sel = (lax.broadcasted_iota(jnp.int32, (r, r), 1).astype(F32) == local).astype(BF16)
        acc_scr[...] += _lane_col(gate_ref[...], e) * _dot(sel, y_ref[...])

    @pl.when((e == N_EXPERTS - 1) & (j == 1))
    def _():
        o_ref[...] = _rms(acc_scr[...], gf_ref[...])


def _combine(x, gate, tabs, ys, g_final):
    m, d = x.shape
    r = MOE_BLOCK
    widx = lambda i, e, j: (i * N_EXPERTS + e) * 2 + j
    return pl.pallas_call(
        _combine_kernel,
        out_shape=jax.ShapeDtypeStruct((m, d), F32),
        grid_spec=pltpu.PrefetchScalarGridSpec(
            num_scalar_prefetch=2,
            grid=(m // r, N_EXPERTS, 2),
            in_specs=[pl.BlockSpec((r, d), lambda i, e, j, bt, vt: (i, 0)),
                      pl.BlockSpec((r, 128), lambda i, e, j, bt, vt: (i, 0)),
                      pl.BlockSpec((r, 128), lambda i, e, j, bt, vt: (i, 0)),
                      pl.BlockSpec((r, d), lambda i, e, j, bt, vt: (bt[widx(i, e, j)], 0)),
                      pl.BlockSpec((1, d), lambda i, e, j, bt, vt: (0, 0))],
            out_specs=pl.BlockSpec((r, d), lambda i, e, j, bt, vt: (i, 0)),
            scratch_shapes=[pltpu.VMEM((r, d), F32)]),
        compiler_params=_cparams(("parallel", "arbitrary", "arbitrary"), 48),
        name="moe_combine",
    )(tabs["comb_blk"], tabs["comb_valid"], x, gate, tabs["slot_cols"], ys, g_final)


def _moe(x, g, w_router_pad, wg, wu, wd, g_final, tf):
    m = x.shape[0]
    n_blocks = 2 * m // MOE_BLOCK + N_EXPERTS
    h, gate = _router(x, g, w_router_pad)
    tabs = _route_tables(gate, n_blocks)
    xs = _dispatch(h, tabs, n_blocks)
    ys = _experts(xs, tabs, wg, wu, wd, tf)
    return _combine(x, gate, tabs, ys, g_final)


def _rope_tables(pos, dim, signed):
    half = dim // 2
    inv = ROPE_THETA ** (-jnp.arange(half, dtype=F32) / half)
    ang = pos.astype(F32)[:, None] * inv[None, :]
    c, s = jnp.cos(ang), jnp.sin(ang)
    return jnp.concatenate([c, c], axis=-1), jnp.concatenate([-s if signed else s, s], axis=-1)


def _rot_cols(w):
    half = w.shape[-1] // 2
    return jnp.concatenate([-w[..., half:], w[..., :half]], axis=-1)


def _prep_weights(w_in0, w_gla_gate, b_gla_gate, g_ret_head, g_gla_head, w_out0, w_ffn_gate, w_ffn_up,
                  w_ffn_down, w_dq, w_uq, w_dkv, w_uk, w_uv, w_o_mla, w_router, w_exp_gate, w_exp_up,
                  w_exp_down):
    p = {}
    p["w_in"] = jnp.pad(w_in0, ((0, 0), (0, IN_W_PAD - IN_W))).astype(BF16)
    wgate = w_gla_gate.reshape(GLA_GATE_RANK, N_HEADS_MIX, DK).transpose(1, 0, 2)
    p["wgate"] = jnp.pad(wgate, ((0, 0), (0, 128 - GLA_GATE_RANK), (0, 0))).astype(BF16)
    p["bgate"] = b_gla_gate.reshape(N_HEADS_MIX, 1, DK)
    p["g_ret"] = g_ret_head.reshape(N_HEADS_MIX, 1, DV)
    p["g_gla"] = g_gla_head.reshape(N_HEADS_MIX, 1, DV)
    p["w_out"] = w_out0.astype(BF16)
    p["w_ffn_gate"] = w_ffn_gate.astype(BF16)
    p["w_ffn_up"] = w_ffn_up.astype(BF16)
    p["w_ffn_down"] = w_ffn_down.astype(BF16)
    p["w_dq"] = w_dq.astype(BF16)
    wuq = w_uq.reshape(Q_LORA, MLA_HEADS, QK_NOPE + QK_ROPE).transpose(1, 0, 2)
    p["w_uq_nope"] = wuq[..., :QK_NOPE].astype(BF16)
    p["w_uq_pe"] = wuq[..., QK_NOPE:].astype(BF16)
    p["w_uq_pe_rot"] = _rot_cols(wuq[..., QK_NOPE:]).astype(BF16)
    p["w_uk_t"] = w_uk.transpose(1, 2, 0).astype(BF16)
    p["w_dkv_c"] = w_dkv[:, :KV_LORA].astype(BF16)
    p["w_dkv_pe"] = w_dkv[:, KV_LORA:].astype(BF16)
    p["w_dkv_pe_rot"] = _rot_cols(w_dkv[:, KV_LORA:]).astype(BF16)
    p["w_uv"] = w_uv.transpose(1, 0, 2).astype(BF16)
    p["w_o"] = w_o_mla.astype(BF16)
    p["w_router"] = jnp.pad(w_router, ((0, 0), (0, 128 - N_EXPERTS)))
    p["w_exp_gate"] = w_exp_gate.astype(BF16)
    p["w_exp_up"] = w_exp_up.astype(BF16)
    p["w_exp_down"] = w_exp_down.astype(BF16)
    return p


def _row(v):
    return v.reshape(1, -1)


def _trunk(x, pos, p, norms, *, batch, seq, chunk, s0_ret, s0_gla, attn_fn, act_dtype):
    m = batch * seq
    tm = min(m, 1024)
    cos_r, sin_r = _rope_tables(pos, DK, signed=True)
    proj = _norm_matmul(x, _row(norms["g_norm_mix0"]), p["w_in"], act_dtype, tm, IN_W_PAD // 7)
    mix, ret_s, gla_s = _mixer(proj.reshape(batch, seq, IN_W_PAD), cos_r, sin_r, p["wgate"], p["bgate"],
                               p["g_ret"], p["g_gla"], s0_ret, s0_gla, chunk=chunk, out_dtype=act_dtype)
    x = _resid_matmul(x, mix.reshape(m, 2 * V_W), p["w_out"], min(m, 512))
    x = _ffn(x, _row(norms["g_norm_ffn0"]), p["w_ffn_gate"], p["w_ffn_up"], p["w_ffn_down"], tm, 256)
    cos_m, sin_m = _rope_tables(pos, QK_ROPE, signed=False)
    reps = max(ATT_TILE // seq, 1)
    cos_m, sin_m = jnp.tile(cos_m, (reps, 1)), jnp.tile(sin_m, (reps, 1))
    q_lat, q_pe, ckv, kpe = _mla_prep(x, _row(norms["g_norm_mix1"]), p["w_dq"], _row(norms["g_q"]),
                                      p["w_uq_nope"], p["w_uq_pe"], p["w_uq_pe_rot"], p["w_uk_t"],
                                      p["w_dkv_c"], p["w_dkv_pe"], p["w_dkv_pe_rot"], _row(norms["g_kv"]),
                                      cos_m, sin_m, act_dtype)
    o_lat = attn_fn(q_lat, q_pe, ckv, kpe)
    x = _attn_out(x, o_lat, p["w_uv"], p["w_o"])
    y = _moe(x, _row(norms["g_norm_ffn1"]), p["w_router"], p["w_exp_gate"], p["w_exp_up"], p["w_exp_down"],
             _row(norms["g_final"]), 512)
    return y, ret_s, gla_s, ckv, kpe


def kernel(x_prompt, x_sample, state_ret, state_gla, cache_ckv, cache_kpe, page_table, g_norm_mix0, w_in0, w_gla_gate, b_gla_gate, g_ret_head, g_gla_head, w_out0, g_norm_ffn0, w_ffn_gate, w_ffn_up, w_ffn_down, g_norm_mix1, w_dq, g_q, w_uq, w_dkv, g_kv, w_uk, w_uv, w_o_mla, g_norm_ffn1, w_router, w_exp_gate, w_exp_up, w_exp_down, g_final):
    batch, seq, _ = x_prompt.shape
    dec_batch, dec_seq, _ = x_sample.shape
    p = _prep_weights(w_in0, w_gla_gate, b_gla_gate, g_ret_head, g_gla_head, w_out0, w_ffn_gate, w_ffn_up,
                      w_ffn_down, w_dq, w_uq, w_dkv, w_uk, w_uv, w_o_mla, w_router, w_exp_gate, w_exp_up,
                      w_exp_down)
    norms = dict(g_norm_mix0=g_norm_mix0, g_norm_ffn0=g_norm_ffn0, g_norm_mix1=g_norm_mix1, g_q=g_q,
                 g_kv=g_kv, g_norm_ffn1=g_norm_ffn1, g_final=g_final)

    yp, ret_p, gla_p, ckv_p, kpe_p = _trunk(
        x_prompt.reshape(batch * seq, D_MODEL), jnp.arange(seq), p, norms,
        batch=batch, seq=seq, chunk=128, s0_ret=None, s0_gla=None,
        attn_fn=lambda ql, qp, c, k: _attn_prompt(ql, qp, c, k, batch, seq), act_dtype=BF16)

    ys, ret_s, gla_s, ckv_s, kpe_s = _trunk(
        x_sample.reshape(dec_batch * dec_seq, D_MODEL), PAST_LEN + jnp.arange(dec_seq), p, norms,
        batch=dec_batch, seq=dec_seq, chunk=dec_seq, s0_ret=state_ret, s0_gla=state_gla,
        attn_fn=lambda ql, qp, c, k: _attn_sample(ql, qp, c, k, cache_ckv, cache_kpe, page_table,
                                                  dec_batch, dec_seq),
        act_dtype=F32)

    return (yp.reshape(batch, seq, D_MODEL), ys.reshape(dec_batch, dec_seq, D_MODEL),
            ret_p, gla_p, ckv_p.reshape(batch, seq, KV_LORA), kpe_p.reshape(batch, seq, QK_ROPE),
            ret_s, gla_s, ckv_s.reshape(dec_batch, dec_seq, KV_LORA), kpe_s.reshape(dec_batch, dec_seq, QK_ROPE))
```

```python
import functools
import math

import numpy as np
import jax
import jax.numpy as jnp
from jax import lax
from jax.experimental import pallas as pl
from jax.experimental.pallas import tpu as pltpu

F32 = jnp.float32
BF16 = jnp.bfloat16

D_MODEL = 1024
PAST_LEN = 8192
PAGE_SIZE = 128
N_HEADS_MIX = 4
DK = 128
DV = 256
GLA_GATE_RANK = 16
GLA_TAU = 16.0
QK_W = N_HEADS_MIX * DK
V_W = N_HEADS_MIX * DV
IN_W = 4 * QK_W + 4 * V_W + GLA_GATE_RANK
IN_W_PAD = 4 * QK_W + 4 * V_W + 128
D_FF = 2816
MLA_HEADS = 16
QK_NOPE = 64
QK_ROPE = 32
V_HEAD = 64
Q_LORA = 384
KV_LORA = 256
MLA_SCALE = (QK_NOPE + QK_ROPE) ** -0.5
N_EXPERTS = 8
D_FF_EXPERT = 3584
ROPE_THETA = 10000.0
EPS = 1e-6

OFF_RQ, OFF_RK, OFF_RV, OFF_RG = 0, QK_W, 2 * QK_W, 2 * QK_W + V_W
OFF_GQ = 2 * QK_W + 2 * V_W
OFF_GK = OFF_GQ + QK_W
OFF_GV = OFF_GK + QK_W
OFF_GR = OFF_GV + V_W
OFF_GA = OFF_GR + V_W

LOG_GAMMA = tuple(math.log1p(-2.0 ** (-5.0 - h)) for h in range(N_HEADS_MIX))
GLA_SAFE_LOG_DECAY = -80.0

ATT_TILE = 256
ATT_HEAD_GROUP = 4


def _cparams(sem, vmem_mb):
    return pltpu.CompilerParams(dimension_semantics=sem, vmem_limit_bytes=vmem_mb << 20)


def _rms(x, g):
    return x * lax.rsqrt(jnp.mean(x * x, axis=-1, keepdims=True) + EPS) * g


def _dot(a, b):
    return jnp.dot(a, b, preferred_element_type=F32)


def _dot_nt(a, b):
    return lax.dot_general(a, b, (((1,), (1,)), ((), ())), preferred_element_type=F32)


def _dot_tn(a, b):
    return lax.dot_general(a, b, (((0,), (0,)), ((), ())), preferred_element_type=F32)


def _silu(x):
    return x / (1.0 + jnp.exp(-x))


def _lane_tile(x, n):
    return x if n == 1 else jnp.concatenate([x] * n, axis=1)


def _pad_rows(x, n):
    if x.shape[0] == n:
        return x
    return jnp.concatenate([x, jnp.zeros((n - x.shape[0], x.shape[1]), x.dtype)], axis=0)


def _norm_matmul_kernel(x_ref, g_ref, w_ref, o_ref, h_scr):
    @pl.when(pl.program_id(1) == 0)
    def _():
        h_scr[...] = _rms(x_ref[...], g_ref[...]).astype(BF16)

    o_ref[...] = _dot(h_scr[...], w_ref[...]).astype(o_ref.dtype)


def _norm_matmul(x, g, w, out_dtype, tm, tn):
    m, d = x.shape
    n = w.shape[1]
    return pl.pallas_call(
        _norm_matmul_kernel,
        out_shape=jax.ShapeDtypeStruct((m, n), out_dtype),
        grid=(m // tm, n // tn),
        in_specs=[pl.BlockSpec((tm, d), lambda i, j: (i, 0)),
                  pl.BlockSpec((1, d), lambda i, j: (0, 0)),
                  pl.BlockSpec((d, tn), lambda i, j: (0, j))],
        out_specs=pl.BlockSpec((tm, tn), lambda i, j: (i, j)),
        scratch_shapes=[pltpu.VMEM((tm, d), BF16)],
        compiler_params=_cparams(("parallel", "arbitrary"), 48),
        name="norm_in_proj",
    )(x, g, w)


def _cumsum_rows(x, n):
    row = lax.broadcasted_iota(jnp.int32, x.shape, 0)
    s = 1
    while s < n:
        x = x + jnp.where(row >= s, pltpu.roll(x, s, 0), 0.0)
        s *= 2
    return x


def _mixer_kernel(*refs, cq, ck, has_state):
    if has_state:
        (proj, cos, sin, wgate, bgate, g_ret, g_gla, s0r, s0g,
         o_ref, ro_ref, go_ref, sr, sg, kf_scr, bc_scr, vf_scr) = refs
    else:
        (proj, cos, sin, wgate, bgate, g_ret, g_gla,
         o_ref, ro_ref, go_ref, sr, sg, kf_scr, bc_scr, vf_scr) = refs
    t = pl.program_id(1)

    @pl.when(t == 0)
    def _init():
        for h in range(N_HEADS_MIX):
            if has_state:
                sr[h] = s0r[0, h]
                sg[h] = s0g[0, h]
            else:
                sr[h] = jnp.zeros((DK, DV), F32)
                sg[h] = jnp.zeros((DK, DV), F32)

    cl = max(cq, 16)

    def lhs(x):
        return _pad_rows(x, cl).astype(BF16)

    def top(y):
        return y if cl == cq else y[:cq]

    row = lax.broadcasted_iota(jnp.int32, (cq, ck), 0)
    col = lax.broadcasted_iota(jnp.int32, (cq, ck), 1)
    causal = row >= col
    relf = jnp.maximum(row - col, 0).astype(F32)
    ridx = lax.broadcasted_iota(jnp.int32, (cq, 1), 0).astype(F32)
    cosv = cos[...]
    sinv = sin[...]

    def rope(x):
        return x * cosv + pltpu.roll(x, DK // 2, 1) * sinv

    def col_slice(off, h, w):
        return proj[0, :, off + h * w: off + (h + 1) * w]

    for h in range(N_HEADS_MIX):
        lg = LOG_GAMMA[h]
        dmask = jnp.where(causal, jnp.exp(lg * relf), 0.0)
        qdec = jnp.exp(lg * (ridx + 1.0))
        kdec = jnp.exp(lg * (cq - 1.0 - ridx))
        cdec = math.exp(lg * cq)
        q = lhs(rope(col_slice(OFF_RQ, h, DK).astype(F32)))
        k = rope(col_slice(OFF_RK, h, DK).astype(F32)) * (DK ** -0.5)
        v = _pad_rows(col_slice(OFF_RV, h, DV).astype(F32), ck).astype(BF16)
        a = top(_dot_nt(q, _pad_rows(k, ck).astype(BF16))) * dmask
        s_old = sr[h]
        o = top(_dot(lhs(a), v)) + top(_dot(q, s_old.astype(BF16))) * qdec
        sr[h] = s_old * cdec + _dot_tn(_pad_rows(k * kdec, ck).astype(BF16), v)
        oc = o - jnp.mean(o, axis=-1, keepdims=True)
        on = oc * lax.rsqrt(jnp.mean(oc * oc, axis=-1, keepdims=True) + EPS) * g_ret[h]
        gate = col_slice(OFF_RG, h, DV).astype(F32)
        o_ref[0, :, h * DV:(h + 1) * DV] = (on * _silu(gate)).astype(o_ref.dtype)

    ga = lhs(proj[0, :, OFF_GA:OFF_GA + 128].astype(F32))
    heads = []
    for h in range(N_HEADS_MIX):
        z = top(_dot(ga, wgate[h])) + bgate[h]
        g = (jnp.minimum(z, 0.0) - jnp.log1p(jnp.exp(-jnp.abs(z)))) * (1.0 / GLA_TAU)
        bc = _cumsum_rows(g, cq)
        blast = bc[cq - 1:cq, :]
        qs = col_slice(OFF_GQ, h, DK).astype(F32) * (DK ** -0.5)
        kf = col_slice(OFF_GK, h, DK).astype(F32)
        vraw = col_slice(OFF_GV, h, DV)
        qe = lhs(qs * jnp.exp(bc))
        vp = _pad_rows(vraw.astype(F32), ck).astype(BF16)
        st_old = sg[h]
        inter = top(_dot(qe, st_old.astype(BF16)))
        kk = _pad_rows(kf * jnp.exp(blast - bc), ck).astype(BF16)
        row_decay = jnp.exp(jnp.broadcast_to(blast, (DK, DK)).T)
        sg[h] = st_old * _lane_tile(row_decay, DV // DK) + _dot_tn(kk, vp)
        heads.append(dict(bc=bc, blast=blast, qs=qs, kf=kf, vraw=vraw, qe=qe, vp=vp, inter=inter))

    def intra_fast():
        outs = []
        for hd in heads:
            kinv = _pad_rows(hd["kf"] * jnp.exp(-hd["bc"]), ck).astype(BF16)
            af = jnp.where(causal, top(_dot_nt(hd["qe"], kinv)), 0.0)
            outs.append(top(_dot(lhs(af), hd["vp"])))
        return tuple(outs)

    def intra_exact():
        outs = []
        for hd in heads:
            kf_scr[...] = hd["kf"]
            bc_scr[...] = hd["bc"]
            vf_scr[...] = hd["vraw"].astype(F32)
            bc, qs = hd["bc"], hd["qs"]

            def body(j, acc, bc=bc, qs=qs):
                kj = kf_scr[pl.ds(j, 1), :]
                bj = bc_scr[pl.ds(j, 1), :]
                vj = vf_scr[pl.ds(j, 1), :]
                w = jnp.exp(jnp.minimum(bc - bj, 0.0))
                c = jnp.sum(qs * kj * w, axis=1, keepdims=True)
                c = jnp.where(ridx >= lax.convert_element_type(j, F32), c, 0.0)
                return acc + c * vj

            outs.append(lax.fori_loop(0, cq, body, jnp.zeros((cq, DV), F32)))
        return tuple(outs)

    min_decay = jnp.min(jnp.concatenate([hd["blast"] for hd in heads], axis=0))
    o_intra = lax.cond(min_decay >= GLA_SAFE_LOG_DECAY, intra_fast, intra_exact)

    for h, hd in enumerate(heads):
        og = o_intra[h] + hd["inter"]
        ogn = og * lax.rsqrt(jnp.mean(og * og, axis=-1, keepdims=True) + EPS) * g_gla[h]
        gateg = col_slice(OFF_GR, h, DV).astype(F32)
        o_ref[0, :, V_W + h * DV:V_W + (h + 1) * DV] = (ogn * _silu(gateg)).astype(o_ref.dtype)

    @pl.when(t == pl.num_programs(1) - 1)
    def _fin():
        for h in range(N_HEADS_MIX):
            ro_ref[0, h] = sr[h]
            go_ref[0, h] = sg[h]


def _mixer(proj, cos, sin, wgate, bgate, g_ret, g_gla, s0_ret, s0_gla, *, chunk, out_dtype):
    b, t, w = proj.shape
    cq = chunk
    ck = max(chunk, 128)
    has_state = s0_ret is not None
    const = lambda *shape: pl.BlockSpec(shape, lambda i, j: (0,) * len(shape))
    state_spec = pl.BlockSpec((1, N_HEADS_MIX, DK, DV), lambda i, j: (i, 0, 0, 0))
    in_specs = [pl.BlockSpec((1, cq, w), lambda i, j: (i, j, 0)),
                pl.BlockSpec((cq, DK), lambda i, j: (j, 0)),
                pl.BlockSpec((cq, DK), lambda i, j: (j, 0)),
                const(N_HEADS_MIX, 128, DK), const(N_HEADS_MIX, 1, DK),
                const(N_HEADS_MIX, 1, DV), const(N_HEADS_MIX, 1, DV)]
    args = [proj, cos, sin, wgate, bgate, g_ret, g_gla]
    if has_state:
        in_specs += [state_spec, state_spec]
        args += [s0_ret, s0_gla]
    return pl.pallas_call(
        functools.partial(_mixer_kernel, cq=cq, ck=ck, has_state=has_state),
        out_shape=(jax.ShapeDtypeStruct((b, t, 2 * V_W), out_dtype),
                   jax.ShapeDtypeStruct((b, N_HEADS_MIX, DK, DV), F32),
                   jax.ShapeDtypeStruct((b, N_HEADS_MIX, DK, DV), F32)),
        grid=(b, t // cq),
        in_specs=in_specs,
        out_specs=(pl.BlockSpec((1, cq, 2 * V_W), lambda i, j: (i, j, 0)), state_spec, state_spec),
        scratch_shapes=[pltpu.VMEM((N_HEADS_MIX, DK, DV), F32), pltpu.VMEM((N_HEADS_MIX, DK, DV), F32),
                        pltpu.VMEM((cq, DK), F32), pltpu.VMEM((cq, DK), F32), pltpu.VMEM((cq, DV), F32)],
        compiler_params=_cparams(("parallel", "arbitrary"), 48),
        name="retention_gla_mixer",
    )(*args)


def _resid_matmul_kernel(x_ref, a_ref, w_ref, o_ref):
    o_ref[...] = x_ref[...] + _dot(a_ref[...].astype(BF16), w_ref[...])


def _resid_matmul(x, a, w, tm):
    m, d = x.shape
    k = a.shape[1]
    return pl.pallas_call(
        _resid_matmul_kernel,
        out_shape=jax.ShapeDtypeStruct((m, d), F32),
        grid=(m // tm,),
        in_specs=[pl.BlockSpec((tm, d), lambda i: (i, 0)),
                  pl.BlockSpec((tm, k), lambda i: (i, 0)),
                  pl.BlockSpec((k, d), lambda i: (0, 0))],
        out_specs=pl.BlockSpec((tm, d), lambda i: (i, 0)),
        compiler_params=_cparams(("parallel",), 48),
        name="resid_out_proj",
    )(x, a, w)


def _ffn_kernel(x_ref, g_ref, wg_ref, wu_ref, wd_ref, o_ref, h_scr, acc_scr):
    f = pl.program_id(1)

    @pl.when(f == 0)
    def _():
        x = x_ref[...]
        h_scr[...] = _rms(x, g_ref[...]).astype(BF16)
        acc_scr[...] = x

    h = h_scr[...]
    act = _silu(_dot(h, wg_ref[...])) * _dot(h, wu_ref[...])
    acc_scr[...] += _dot(act.astype(BF16), wd_ref[...])

    @pl.when(f == pl.num_programs(1) - 1)
    def _():
        o_ref[...] = acc_scr[...]


def _ffn(x, g, wg, wu, wd, tm, tf):
    m, d = x.shape
    f = wg.shape[1]
    return pl.pallas_call(
        _ffn_kernel,
        out_shape=jax.ShapeDtypeStruct((m, d), F32),
        grid=(m // tm, f // tf),
        in_specs=[pl.BlockSpec((tm, d), lambda i, j: (i, 0)),
                  pl.BlockSpec((1, d), lambda i, j: (0, 0)),
                  pl.BlockSpec((d, tf), lambda i, j: (0, j)),
                  pl.BlockSpec((d, tf), lambda i, j: (0, j)),
                  pl.BlockSpec((tf, d), lambda i, j: (j, 0))],
        out_specs=pl.BlockSpec((tm, d), lambda i, j: (i, 0)),
        scratch_shapes=[pltpu.VMEM((tm, d), BF16), pltpu.VMEM((tm, d), F32)],
        compiler_params=_cparams(("parallel", "arbitrary"), 48),
        name="swiglu_ffn",
    )(x, g, wg, wu, wd)


def _mla_prep_kernel(x_ref, g1_ref, wdq_ref, gq_ref, wn_ref, wp_ref, wpr_ref, wuk_ref,
                     wkc_ref, wkp_ref, wkpr_ref, gkv_ref, cos_ref, sin_ref,
                     qlat_ref, qpe_ref, ckv_ref, kpe_ref):
    hn = _rms(x_ref[...], g1_ref[...]).astype(BF16)
    cq = _rms(_dot(hn, wdq_ref[...]), gq_ref[...]).astype(BF16)
    cosv = cos_ref[...]
    sinv = sin_ref[...]
    ckv_ref[...] = _rms(_dot(hn, wkc_ref[...]), gkv_ref[...])
    kpe_ref[...] = _dot(hn, wkp_ref[...]) * cosv + _dot(hn, wkpr_ref[...]) * sinv
    for h in range(MLA_HEADS):
        qn = _dot(cq, wn_ref[h]).astype(BF16)
        qlat_ref[0, h] = _dot(qn, wuk_ref[h]).astype(qlat_ref.dtype)
        qpe_ref[0, h] = (_dot(cq, wp_ref[h]) * cosv + _dot(cq, wpr_ref[h]) * sinv).astype(qpe_ref.dtype)


def _mla_prep(x, g1, wdq, gq, wn, wp, wpr, wuk, wkc, wkp, wkpr, gkv, cos, sin, q_dtype):
    m, d = x.shape
    tm = ATT_TILE
    nt = m // tm
    n_pos_tiles = cos.shape[0] // tm
    full = lambda a: pl.BlockSpec(a.shape, lambda i: (0,) * a.ndim)
    return pl.pallas_call(
        _mla_prep_kernel,
        out_shape=(jax.ShapeDtypeStruct((nt, MLA_HEADS, tm, KV_LORA), q_dtype),
                   jax.ShapeDtypeStruct((nt, MLA_HEADS, tm, QK_ROPE), q_dtype),
                   jax.ShapeDtypeStruct((m, KV_LORA), F32),
                   jax.ShapeDtypeStruct((m, QK_ROPE), F32)),
        grid=(nt,),
        in_specs=[pl.BlockSpec((tm, d), lambda i: (i, 0)), full(g1), full(wdq), full(gq), full(wn), full(wp),
                  full(wpr), full(wuk), full(wkc), full(wkp), full(wkpr), full(gkv),
                  pl.BlockSpec((tm, QK_ROPE), lambda i: (i % n_pos_tiles, 0)),
                  pl.BlockSpec((tm, QK_ROPE), lambda i: (i % n_pos_tiles, 0))],
        out_specs=(pl.BlockSpec((1, MLA_HEADS, tm, KV_LORA), lambda i: (i, 0, 0, 0)),
                   pl.BlockSpec((1, MLA_HEADS, tm, QK_ROPE), lambda i: (i, 0, 0, 0)),
                   pl.BlockSpec((tm, KV_LORA), lambda i: (i, 0)),
                   pl.BlockSpec((tm, QK_ROPE), lambda i: (i, 0))),
        compiler_params=_cparams(("parallel",), 48),
        name="mla_prep",
    )(x, g1, wdq, gq, wn, wp, wpr, wuk, wkc, wkp, wkpr, gkv, cos, sin)


def _attn_prompt_kernel(it_ref, jt_ref, q_ref, qpe_ref, ckv_ref, kpe_ref, o_ref,
                        kc_scr, kp_scr, m_scr, l_scr, acc_scr):
    p = pl.program_id(1)
    i = it_ref[p]
    j = jt_ref[p]
    tq = ATT_TILE
    hb = ATT_HEAD_GROUP
    rows = hb * tq

    @pl.when(j == 0)
    def _():
        m_scr[...] = jnp.full(m_scr.shape, -jnp.inf, F32)
        l_scr[...] = jnp.zeros(l_scr.shape, F32)
        acc_scr[...] = jnp.zeros(acc_scr.shape, F32)

    kc_scr[...] = ckv_ref[0].astype(BF16)
    kp_scr[...] = kpe_ref[0].astype(BF16)

    def make_body(masked):
        def body(gidx, carry):
            hs = pl.ds(gidx * hb, hb)
            q = q_ref[0, hs].reshape(rows, KV_LORA)
            qp = qpe_ref[0, hs].reshape(rows, QK_ROPE)
            kc = kc_scr[...]
            s = (_dot_nt(q, kc) + _dot_nt(qp, kp_scr[...])) * MLA_SCALE
            if masked:
                qpos = lax.broadcasted_iota(jnp.int32, (rows, tq), 0) & (tq - 1)
                s = jnp.where(lax.broadcasted_iota(jnp.int32, (rows, tq), 1) <= qpos, s, -jnp.inf)
            m_prev = m_scr[hs].reshape(rows, 128)
            m_new = jnp.maximum(m_prev, jnp.max(s, axis=1, keepdims=True))
            alpha = jnp.exp(m_prev - m_new)
            pr = jnp.exp(s - _lane_tile(m_new, tq // 128))
            l_new = alpha * l_scr[hs].reshape(rows, 128) + jnp.sum(pr, axis=1, keepdims=True)
            l_scr[hs] = l_new.reshape(hb, tq, 128)
            m_scr[hs] = m_new.reshape(hb, tq, 128)
            acc = (acc_scr[hs].reshape(rows, KV_LORA) * _lane_tile(alpha, KV_LORA // 128)
                   + _dot(pr.astype(BF16), kc))
            acc_scr[hs] = acc.reshape(hb, tq, KV_LORA)
            return carry
        return body

    @pl.when(j < i)
    def _():
        lax.fori_loop(0, MLA_HEADS // hb, make_body(False), 0)

    @pl.when(j == i)
    def _():
        lax.fori_loop(0, MLA_HEADS // hb, make_body(True), 0)
        inv_l = 1.0 / l_scr[...].reshape(MLA_HEADS * tq, 128)
        acc = acc_scr[...].reshape(MLA_HEADS * tq, KV_LORA) * _lane_tile(inv_l, KV_LORA // 128)
        o_ref[0] = acc.reshape(MLA_HEADS, tq, KV_LORA).astype(o_ref.dtype)


def _attn_prompt(q_lat, q_pe, ckv, kpe, batch, seq):
    tq = ATT_TILE
    nq = seq // tq
    pairs = [(i, j) for i in range(nq) for j in range(i + 1)]
    i_tab = jnp.asarray(np.array([p[0] for p in pairs], np.int32))
    j_tab = jnp.asarray(np.array([p[1] for p in pairs], np.int32))
    ckv3 = ckv.reshape(batch, seq, KV_LORA)
    kpe3 = kpe.reshape(batch, seq, QK_ROPE)
    return pl.pallas_call(
        _attn_prompt_kernel,
        out_shape=jax.ShapeDtypeStruct(q_lat.shape, BF16),
        grid_spec=pltpu.PrefetchScalarGridSpec(
            num_scalar_prefetch=2,
            grid=(batch, len(pairs)),
            in_specs=[pl.BlockSpec((1, MLA_HEADS, tq, KV_LORA), lambda b, p, it, jt: (b * nq + it[p], 0, 0, 0)),
                      pl.BlockSpec((1, MLA_HEADS, tq, QK_ROPE), lambda b, p, it, jt: (b * nq + it[p], 0, 0, 0)),
                      pl.BlockSpec((1, tq, KV_LORA), lambda b, p, it, jt: (b, jt[p], 0)),
                      pl.BlockSpec((1, tq, QK_ROPE), lambda b, p, it, jt: (b, jt[p], 0))],
            out_specs=pl.BlockSpec((1, MLA_HEADS, tq, KV_LORA), lambda b, p, it, jt: (b * nq + it[p], 0, 0, 0)),
            scratch_shapes=[pltpu.VMEM((tq, KV_LORA), BF16), pltpu.VMEM((tq, QK_ROPE), BF16),
                            pltpu.VMEM((MLA_HEADS, tq, 128), F32), pltpu.VMEM((MLA_HEADS, tq, 128), F32),
                            pltpu.VMEM((MLA_HEADS, tq, KV_LORA), F32)]),
        compiler_params=_cparams(("parallel", "arbitrary"), 48),
        name="mla_prompt_attention",
    )(i_tab, j_tab, q_lat, q_pe, ckv3, kpe3)


def _attn_sample_kernel(*refs, n_pages_step, dec_seq):
    g = n_pages_step
    pt_ref, q_ref, qpe_ref, cn_ref, pn_ref = refs[:5]
    c_refs = refs[5:5 + g]
    p_refs = refs[5 + g:5 + 2 * g]
    o_ref, m_scr, l_scr, acc_scr = refs[5 + 2 * g:]
    del pt_ref
    s_idx = pl.program_id(1)
    rows = MLA_HEADS * dec_seq
    q = q_ref[...].reshape(rows, KV_LORA).astype(BF16)
    qp = qpe_ref[...].reshape(rows, QK_ROPE).astype(BF16)

    def update(s, kc, mask):
        s = s * MLA_SCALE
        if mask is not None:
            s = jnp.where(mask, s, -jnp.inf)
        m_prev = m_scr[...]
        m_new = jnp.maximum(m_prev, jnp.max(s, axis=1, keepdims=True))
        alpha = jnp.exp(m_prev - m_new)
        pr = jnp.exp(s - _lane_tile(m_new, s.shape[1] // 128))
        l_scr[...] = alpha * l_scr[...] + jnp.sum(pr, axis=1, keepdims=True)
        m_scr[...] = m_new
        acc_scr[...] = acc_scr[...] * _lane_tile(alpha, KV_LORA // 128) + _dot(pr.astype(BF16), kc)

    @pl.when(s_idx == 0)
    def _():
        m_scr[...] = jnp.full(m_scr.shape, -jnp.inf, F32)
        l_scr[...] = jnp.zeros(l_scr.shape, F32)
        acc_scr[...] = jnp.zeros(acc_scr.shape, F32)
        kc = _pad_rows(cn_ref[0], 128).astype(BF16)
        kp = _pad_rows(pn_ref[0], 128).astype(BF16)
        tpos = lax.broadcasted_iota(jnp.int32, (rows, 128), 0) & (dec_seq - 1)
        update(_dot_nt(q, kc) + _dot_nt(qp, kp), kc, lax.broadcasted_iota(jnp.int32, (rows, 128), 1) <= tpos)

    kc = jnp.concatenate([r[0] for r in c_refs], axis=0).astype(BF16)
    kpt = jnp.concatenate([r[0] for r in p_refs], axis=1).astype(BF16)
    update(_dot_nt(q, kc) + _dot(qp, kpt), kc, None)

    @pl.when(s_idx == pl.num_programs(1) - 1)
    def _():
        out = acc_scr[...] * _lane_tile(1.0 / l_scr[...], KV_LORA // 128)
        o_ref[...] = out.reshape(o_ref.shape).astype(o_ref.dtype)


def _attn_sample(q_lat, q_pe, ckv_new, kpe_new, cache_ckv, cache_kpe, page_table, dec_batch, dec_seq,
                 n_pages_step=32):
    tq = ATT_TILE
    nt = q_lat.shape[0]
    bpt = tq // dec_seq
    n_pages = page_table.shape[1]
    g = n_pages_step
    q5 = q_lat.reshape(nt, MLA_HEADS, bpt, dec_seq, KV_LORA)
    qp5 = q_pe.reshape(nt, MLA_HEADS, bpt, dec_seq, QK_ROPE)
    cn = ckv_new.reshape(dec_batch, dec_seq, KV_LORA)
    pn = kpe_new.reshape(dec_batch, dec_seq, QK_ROPE)
    q_spec = lambda w: pl.BlockSpec((1, MLA_HEADS, 1, dec_seq, w), lambda b, s, pt: (b // bpt, 0, b % bpt, 0, 0))
    page_spec = lambda shape, gi: pl.BlockSpec((1,) + shape,
                                               lambda b, s, pt: (pt[b * n_pages + s * g + gi], 0, 0))
    in_specs = ([q_spec(KV_LORA), q_spec(QK_ROPE),
                 pl.BlockSpec((1, dec_seq, KV_LORA), lambda b, s, pt: (b, 0, 0)),
                 pl.BlockSpec((1, dec_seq, QK_ROPE), lambda b, s, pt: (b, 0, 0))]
                + [page_spec((PAGE_SIZE, KV_LORA), gi) for gi in range(g)]
                + [page_spec((QK_ROPE, PAGE_SIZE), gi) for gi in range(g)])
    cache_kpe_t = jnp.swapaxes(cache_kpe, 1, 2)
    rows = MLA_HEADS * dec_seq
    out = pl.pallas_call(
        functools.partial(_attn_sample_kernel, n_pages_step=g, dec_seq=dec_seq),
        out_shape=jax.ShapeDtypeStruct(q5.shape, F32),
        grid_spec=pltpu.PrefetchScalarGridSpec(
            num_scalar_prefetch=1,
            grid=(dec_batch, n_pages // g),
            in_specs=in_specs,
            out_specs=q_spec(KV_LORA),
            scratch_shapes=[pltpu.VMEM((rows, 128), F32), pltpu.VMEM((rows, 128), F32),
                            pltpu.VMEM((rows, KV_LORA), F32)]),
        compiler_params=_cparams(("parallel", "arbitrary"), 48),
        name="mla_sample_attention",
    )(page_table.reshape(-1), q5, qp5, cn, pn, *([cache_ckv] * g), *([cache_kpe_t] * g))
    return out.reshape(q_lat.shape)


def _attn_out_kernel(x_ref, o_ref_in, wuv_ref, wo_ref, y_ref, cat_scr):
    for h in range(MLA_HEADS):
        cat_scr[:, h * V_HEAD:(h + 1) * V_HEAD] = _dot(o_ref_in[0, h].astype(BF16), wuv_ref[h])
    y_ref[...] = x_ref[...] + _dot(cat_scr[...].astype(BF16), wo_ref[...])


def _attn_out(x, o_lat, wuv, wo):
    m, d = x.shape
    tm = ATT_TILE
    return pl.pallas_call(
        _attn_out_kernel,
        out_shape=jax.ShapeDtypeStruct((m, d), F32),
        grid=(m // tm,),
        in_specs=[pl.BlockSpec((tm, d), lambda i: (i, 0)),
                  pl.BlockSpec((1, MLA_HEADS, tm, KV_LORA), lambda i: (i, 0, 0, 0)),
                  pl.BlockSpec(wuv.shape, lambda i: (0, 0, 0)),
                  pl.BlockSpec(wo.shape, lambda i: (0, 0))],
        out_specs=pl.BlockSpec((tm, d), lambda i: (i, 0)),
        scratch_shapes=[pltpu.VMEM((tm, MLA_HEADS * V_HEAD), F32)],
        compiler_params=_cparams(("parallel",), 48),
        name="mla_out_proj",
    )(x, o_lat, wuv, wo)


MOE_BLOCK = 512
MOE_WINDOW = 128


def _lane_col(table, e):
    lane = lax.broadcasted_iota(jnp.int32, table.shape, 1)
    return jnp.sum(jnp.where(lane == e, table, 0.0), axis=1, keepdims=True)


def _router_kernel(x_ref, g_ref, wr_ref, h_ref, gate_ref):
    tm = x_ref.shape[0]
    lane = lax.broadcasted_iota(jnp.int32, (tm, 128), 1).astype(F32)
    hn = _rms(x_ref[...], g_ref[...])
    h_ref[...] = hn.astype(BF16)
    logits = jnp.dot(hn, wr_ref[...], preferred_element_type=F32, precision=lax.Precision.HIGHEST)
    logits = jnp.where(lane < N_EXPERTS, logits, -jnp.inf)
    m1 = jnp.max(logits, axis=1, keepdims=True)
    i1 = jnp.min(jnp.where(logits == m1, lane, 128.0), axis=1, keepdims=True)
    rest = jnp.where(lane == i1, -jnp.inf, logits)
    m2 = jnp.max(rest, axis=1, keepdims=True)
    i2 = jnp.min(jnp.where(rest == m2, lane, 128.0), axis=1, keepdims=True)
    e2 = jnp.exp(m2 - m1)
    w1 = 1.0 / (1.0 + e2)
    gate_ref[...] = jnp.where(lane == i1, w1, 0.0) + jnp.where(lane == i2, e2 * w1, 0.0)


def _router(x, g, w_router_pad):
    m, d = x.shape
    tm = MOE_BLOCK
    return pl.pallas_call(
        _router_kernel,
        out_shape=(jax.ShapeDtypeStruct((m, d), BF16), jax.ShapeDtypeStruct((m, 128), F32)),
        grid=(m // tm,),
        in_specs=[pl.BlockSpec((tm, d), lambda i: (i, 0)),
                  pl.BlockSpec((1, d), lambda i: (0, 0)),
                  pl.BlockSpec((d, 128), lambda i: (0, 0))],
        out_specs=(pl.BlockSpec((tm, d), lambda i: (i, 0)), pl.BlockSpec((tm, 128), lambda i: (i, 0))),
        compiler_params=_cparams(("parallel",), 48),
        name="moe_router",
    )(x, g, w_router_pad)


def _route_tables(gate, n_blocks):
    r = MOE_BLOCK
    m = gate.shape[0]
    nt = m // r
    sel = gate[:, :N_EXPERTS] > 0.0
    csum = jnp.cumsum(sel.astype(jnp.int32), axis=0)
    nblk = (csum[-1] + r - 1) // r
    blk_end = jnp.cumsum(nblk)
    blk_start = blk_end - nblk
    slot = jnp.where(sel, blk_start[None, :] * r + csum - 1, -1)
    block_expert = jnp.minimum(jnp.sum(jnp.arange(n_blocks, dtype=jnp.int32)[:, None] >= blk_end[None, :], axis=1),
                               N_EXPERTS - 1).astype(jnp.int32)
    r1 = csum[r - 1::r]
    r0 = jnp.concatenate([jnp.zeros((1, N_EXPERTS), jnp.int32), r1[:-1]], axis=0)
    has = r1 > r0
    g0 = blk_start[None, :] * r + r0
    g1 = blk_start[None, :] * r + r1
    b0 = g0 // r
    b1 = jnp.maximum(g1 - 1, 0) // r
    blk = jnp.stack([b0, jnp.where(has & (b1 > b0), b1, b0)], axis=-1)
    valid = jnp.stack([has, has & (b1 > b0)], axis=-1)
    blk = jnp.clip(blk, 0, n_blocks - 1)
    lo = jnp.maximum(g0[..., None], blk * r) - blk * r
    hi = jnp.minimum(g1[..., None], (blk + 1) * r) - blk * r
    win_lo = lo // MOE_WINDOW
    win_n = jnp.where(valid, (hi - 1) // MOE_WINDOW - win_lo + 1, 0)
    flat_c = lambda a: a.reshape(-1).astype(jnp.int32)
    flat_d = lambda a: a.transpose(1, 0, 2).reshape(-1).astype(jnp.int32)
    d_valid = flat_d(valid)
    d_raw = jnp.where(d_valid == 1, flat_d(blk), -1)
    n_steps = d_raw.shape[0]
    earlier = jnp.arange(n_steps)[None, :] <= jnp.arange(n_steps)[:, None]
    seen = jnp.max(jnp.where(earlier, d_raw[None, :], -1), axis=1)
    prev = jnp.concatenate([jnp.full((1,), -1, jnp.int32), seen[:-1]])
    d_first = ((d_valid == 1) & (seen != prev)).astype(jnp.int32)
    slot_f = slot.astype(F32)
    slot_cols = jnp.pad(slot_f, ((0, 0), (0, 128 - N_EXPERTS)), constant_values=-1.0)
    return dict(slot_rows=slot_f.T, slot_cols=slot_cols, block_expert=block_expert,
                n_used=blk_end[-1:].astype(jnp.int32),
                c_blk=flat_c(blk), c_lo=flat_c(win_lo), c_n=flat_c(win_n),
                d_blk=jnp.maximum(seen, 0), d_lo=flat_d(win_lo), d_n=flat_d(win_n), d_first=d_first)


def _dispatch_kernel(blk_ref, lo_ref, n_ref, first_ref, h_ref, slot_ref, o_ref):
    e, i, j = pl.program_id(0), pl.program_id(1), pl.program_id(2)
    w = (e * pl.num_programs(1) + i) * 2 + j
    r, wn = MOE_BLOCK, MOE_WINDOW

    @pl.when(first_ref[w] == 1)
    def _():
        o_ref[...] = jnp.zeros(o_ref.shape, o_ref.dtype)

    local = slot_ref[pl.ds(e, 1), :] - lax.convert_element_type(blk_ref[w] * r, F32)
    h = h_ref[...]

    def body(t, carry):
        off = pl.multiple_of((lo_ref[w] + t) * wn, wn)
        srow = lax.broadcasted_iota(jnp.int32, (wn, r), 0).astype(F32) + lax.convert_element_type(off, F32)
        rows = _dot((srow == local).astype(BF16), h)
        o_ref[pl.ds(off, wn), :] = o_ref[pl.ds(off, wn), :] + rows.astype(o_ref.dtype)
        return carry

    lax.fori_loop(0, n_ref[w], body, 0)


def _dispatch(h, tabs, n_blocks):
    m, d = h.shape
    r = MOE_BLOCK
    nt = m // r
    widx = lambda e, i, j: (e * nt + i) * 2 + j
    return pl.pallas_call(
        _dispatch_kernel,
        out_shape=jax.ShapeDtypeStruct((n_blocks * r, d), BF16),
        grid_spec=pltpu.PrefetchScalarGridSpec(
            num_scalar_prefetch=4,
            grid=(N_EXPERTS, nt, 2),
            in_specs=[pl.BlockSpec((r, d), lambda e, i, j, *tabs: (i, 0)),
                      pl.BlockSpec((N_EXPERTS, r), lambda e, i, j, *tabs: (0, i))],
            out_specs=pl.BlockSpec((r, d), lambda e, i, j, bt, *tabs: (bt[widx(e, i, j)], 0))),
        compiler_params=_cparams(("arbitrary", "arbitrary", "arbitrary"), 48),
        name="moe_dispatch",
    )(tabs["d_blk"], tabs["d_lo"], tabs["d_n"], tabs["d_first"], h, tabs["slot_rows"])


def _experts_kernel(be_ref, nu_ref, x_ref, wg_ref, wu_ref, wd_ref, o_ref, acc_scr):
    del be_ref
    k, f = pl.program_id(0), pl.program_id(1)

    @pl.when(k < nu_ref[0])
    def _():
        x = x_ref[...]
        act = _silu(_dot(x, wg_ref[0])) * _dot(x, wu_ref[0])
        part = _dot(act.astype(BF16), wd_ref[0])

        @pl.when(f == 0)
        def _():
            acc_scr[...] = part

        @pl.when(f > 0)
        def _():
            acc_scr[...] += part

        @pl.when(f == pl.num_programs(1) - 1)
        def _():
            o_ref[...] = acc_scr[...].astype(o_ref.dtype)


def _experts(xs, tabs, wg, wu, wd, tf):
    rows, d = xs.shape
    r = MOE_BLOCK
    nb = rows // r
    nf = wg.shape[2] // tf
    kk = lambda k, nu: jnp.minimum(k, nu[0] - 1)
    jj = lambda k, j, nu: jnp.where(k < nu[0], j, nf - 1)
    return pl.pallas_call(
        _experts_kernel,
        out_shape=jax.ShapeDtypeStruct((rows, d), BF16),
        grid_spec=pltpu.PrefetchScalarGridSpec(
            num_scalar_prefetch=2,
            grid=(nb, nf),
            in_specs=[pl.BlockSpec((r, d), lambda k, j, be, nu: (kk(k, nu), 0)),
                      pl.BlockSpec((1, d, tf), lambda k, j, be, nu: (be[kk(k, nu)], 0, jj(k, j, nu))),
                      pl.BlockSpec((1, d, tf), lambda k, j, be, nu: (be[kk(k, nu)], 0, jj(k, j, nu))),
                      pl.BlockSpec((1, tf, d), lambda k, j, be, nu: (be[kk(k, nu)], jj(k, j, nu), 0))],
            out_specs=pl.BlockSpec((r, d), lambda k, j, be, nu: (kk(k, nu), 0)),
            scratch_shapes=[pltpu.VMEM((r, d), F32)]),
        compiler_params=_cparams(("arbitrary", "arbitrary"), 48),
        name="moe_experts",
    )(tabs["block_expert"], tabs["n_used"], xs, wg, wu, wd)


def _combine_kernel(blk_ref, lo_ref, n_ref, x_ref, gate_ref, slot_ref, y_ref, gf_ref, o_ref, acc_scr):
    i, e, j = pl.program_id(0), pl.program_id(1), pl.program_id(2)
    w = (i * N_EXPERTS + e) * 2 + j
    r, wn = MOE_BLOCK, MOE_WINDOW

    @pl.when((e == 0) & (j == 0))
    def _():
        acc_scr[...] = x_ref[...]

    @pl.when(n_ref[w] > 0)
    def _():
        local = _lane_col(slot_ref[...], e) - lax.convert_element_type(blk_ref[w] * r, F32)
        gcol = _lane_col(gate_ref[...], e)

        def body(t, carry):
            off = pl.multiple_of((lo_ref[w] + t) * wn, wn)
            scol = lax.broadcasted_iota(jnp.int32, (r, wn), 1).astype(F32) + lax.convert_element_type(off, F32)
            acc_scr[...] += gcol * _dot((scol == local).astype(BF16), y_ref[pl.ds(off, wn), :])
            return carry

        lax.fori_loop(0, n_ref[w], body, 0)

    @pl.when((e == N_EXPERTS - 1) & (j == 1))
    def _():
        o_ref[...] = _rms(acc_scr[...], gf_ref[...])


def _combine(x, gate, tabs, ys, g_final):
    m, d = x.shape
    r = MOE_BLOCK
    widx = lambda i, e, j: (i * N_EXPERTS + e) * 2 + j
    return pl.pallas_call(
        _combine_kernel,
        out_shape=jax.ShapeDtypeStruct((m, d), F32),
        grid_spec=pltpu.PrefetchScalarGridSpec(
            num_scalar_prefetch=3,
            grid=(m // r, N_EXPERTS, 2),
            in_specs=[pl.BlockSpec((r, d), lambda i, e, j, *tabs: (i, 0)),
                      pl.BlockSpec((r, 128), lambda i, e, j, *tabs: (i, 0)),
                      pl.BlockSpec((r, 128), lambda i, e, j, *tabs: (i, 0)),
                      pl.BlockSpec((r, d), lambda i, e, j, bt, *tabs: (bt[widx(i, e, j)], 0)),
                      pl.BlockSpec((1, d), lambda i, e, j, *tabs: (0, 0))],
            out_specs=pl.BlockSpec((r, d), lambda i, e, j, *tabs: (i, 0)),
            scratch_shapes=[pltpu.VMEM((r, d), F32)]),
        compiler_params=_cparams(("parallel", "arbitrary", "arbitrary"), 48),
        name="moe_combine",
    )(tabs["c_blk"], tabs["c_lo"], tabs["c_n"], x, gate, tabs["slot_cols"], ys, g_final)


def _moe(x, g, w_router_pad, wg, wu, wd, g_final, tf):
    m = x.shape[0]
    n_blocks = 2 * m // MOE_BLOCK + N_EXPERTS
    h, gate = _router(x, g, w_router_pad)
    tabs = _route_tables(gate, n_blocks)
    xs = _dispatch(h, tabs, n_blocks)
    ys = _experts(xs, tabs, wg, wu, wd, tf)
    return _combine(x, gate, tabs, ys, g_final)


def _rope_tables(pos, dim, signed):
    half = dim // 2
    inv = ROPE_THETA ** (-jnp.arange(half, dtype=F32) / half)
    ang = pos.astype(F32)[:, None] * inv[None, :]
    c, s = jnp.cos(ang), jnp.sin(ang)
    return jnp.concatenate([c, c], axis=-1), jnp.concatenate([-s if signed else s, s], axis=-1)


def _rot_cols(w):
    half = w.shape[-1] // 2
    return jnp.concatenate([-w[..., half:], w[..., :half]], axis=-1)


def _prep_weights(w_in0, w_gla_gate, b_gla_gate, g_ret_head, g_gla_head, w_out0, w_ffn_gate, w_ffn_up,
                  w_ffn_down, w_dq, w_uq, w_dkv, w_uk, w_uv, w_o_mla, w_router, w_exp_gate, w_exp_up,
                  w_exp_down):
    p = {}
    p["w_in"] = jnp.pad(w_in0, ((0, 0), (0, IN_W_PAD - IN_W))).astype(BF16)
    wgate = w_gla_gate.reshape(GLA_GATE_RANK, N_HEADS_MIX, DK).transpose(1, 0, 2)
    p["wgate"] = jnp.pad(wgate, ((0, 0), (0, 128 - GLA_GATE_RANK), (0, 0))).astype(BF16)
    p["bgate"] = b_gla_gate.reshape(N_HEADS_MIX, 1, DK)
    p["g_ret"] = g_ret_head.reshape(N_HEADS_MIX, 1, DV)
    p["g_gla"] = g_gla_head.reshape(N_HEADS_MIX, 1, DV)
    p["w_out"] = w_out0.astype(BF16)
    p["w_ffn_gate"] = w_ffn_gate.astype(BF16)
    p["w_ffn_up"] = w_ffn_up.astype(BF16)
    p["w_ffn_down"] = w_ffn_down.astype(BF16)
    p["w_dq"] = w_dq.astype(BF16)
    wuq = w_uq.reshape(Q_LORA, MLA_HEADS, QK_NOPE + QK_ROPE).transpose(1, 0, 2)
    p["w_uq_nope"] = wuq[..., :QK_NOPE].astype(BF16)
    p["w_uq_pe"] = wuq[..., QK_NOPE:].astype(BF16)
    p["w_uq_pe_rot"] = _rot_cols(wuq[..., QK_NOPE:]).astype(BF16)
    p["w_uk_t"] = w_uk.transpose(1, 2, 0).astype(BF16)
    p["w_dkv_c"] = w_dkv[:, :KV_LORA].astype(BF16)
    p["w_dkv_pe"] = w_dkv[:, KV_LORA:].astype(BF16)
    p["w_dkv_pe_rot"] = _rot_cols(w_dkv[:, KV_LORA:]).astype(BF16)
    p["w_uv"] = w_uv.transpose(1, 0, 2).astype(BF16)
    p["w_o"] = w_o_mla.astype(BF16)
    p["w_router"] = jnp.pad(w_router, ((0, 0), (0, 128 - N_EXPERTS)))
    p["w_exp_gate"] = w_exp_gate.astype(BF16)
    p["w_exp_up"] = w_exp_up.astype(BF16)
    p["w_exp_down"] = w_exp_down.astype(BF16)
    return p


def _row(v):
    return v.reshape(1, -1)


def _trunk(x, pos, p, norms, *, batch, seq, chunk, s0_ret, s0_gla, attn_fn, act_dtype):
    m = batch * seq
    tm = min(m, 1024)
    cos_r, sin_r = _rope_tables(pos, DK, signed=True)
    proj = _norm_matmul(x, _row(norms["g_norm_mix0"]), p["w_in"], act_dtype, tm, IN_W_PAD // 7)
    mix, ret_s, gla_s = _mixer(proj.reshape(batch, seq, IN_W_PAD), cos_r, sin_r, p["wgate"], p["bgate"],
                               p["g_ret"], p["g_gla"], s0_ret, s0_gla, chunk=chunk, out_dtype=act_dtype)
    x = _resid_matmul(x, mix.reshape(m, 2 * V_W), p["w_out"], min(m, 512))
    x = _ffn(x, _row(norms["g_norm_ffn0"]), p["w_ffn_gate"], p["w_ffn_up"], p["w_ffn_down"], tm, 256)
    cos_m, sin_m = _rope_tables(pos, QK_ROPE, signed=False)
    reps = max(ATT_TILE // seq, 1)
    cos_m, sin_m = jnp.tile(cos_m, (reps, 1)), jnp.tile(sin_m, (reps, 1))
    q_lat, q_pe, ckv, kpe = _mla_prep(x, _row(norms["g_norm_mix1"]), p["w_dq"], _row(norms["g_q"]),
                                      p["w_uq_nope"], p["w_uq_pe"], p["w_uq_pe_rot"], p["w_uk_t"],
                                      p["w_dkv_c"], p["w_dkv_pe"], p["w_dkv_pe_rot"], _row(norms["g_kv"]),
                                      cos_m, sin_m, act_dtype)
    o_lat = attn_fn(q_lat, q_pe, ckv, kpe)
    x = _attn_out(x, o_lat, p["w_uv"], p["w_o"])
    y = _moe(x, _row(norms["g_norm_ffn1"]), p["w_router"], p["w_exp_gate"], p["w_exp_up"], p["w_exp_down"],
             _row(norms["g_final"]), 512)
    return y, ret_s, gla_s, ckv, kpe


def kernel(x_prompt, x_sample, state_ret, state_gla, cache_ckv, cache_kpe, page_table, g_norm_mix0, w_in0, w_gla_gate, b_gla_gate, g_ret_head, g_gla_head, w_out0, g_norm_ffn0, w_ffn_gate, w_ffn_up, w_ffn_down, g_norm_mix1, w_dq, g_q, w_uq, w_dkv, g_kv, w_uk, w_uv, w_o_mla, g_norm_ffn1, w_router, w_exp_gate, w_exp_up, w_exp_down, g_final):
    batch, seq, _ = x_prompt.shape
    dec_batch, dec_seq, _ = x_sample.shape
    p = _prep_weights(w_in0, w_gla_gate, b_gla_gate, g_ret_head, g_gla_head, w_out0, w_ffn_gate, w_ffn_up,
                      w_ffn_down, w_dq, w_uq, w_dkv, w_uk, w_uv, w_o_mla, w_router, w_exp_gate, w_exp_up,
                      w_exp_down)
    norms = dict(g_norm_mix0=g_norm_mix0, g_norm_ffn0=g_norm_ffn0, g_norm_mix1=g_norm_mix1, g_q=g_q,
                 g_kv=g_kv, g_norm_ffn1=g_norm_ffn1, g_final=g_final)

    yp, ret_p, gla_p, ckv_p, kpe_p = _trunk(
        x_prompt.reshape(batch * seq, D_MODEL), jnp.arange(seq), p, norms,
        batch=batch, seq=seq, chunk=128, s0_ret=None, s0_gla=None,
        attn_fn=lambda ql, qp, c, k: _attn_prompt(ql, qp, c, k, batch, seq), act_dtype=BF16)

    ys, ret_s, gla_s, ckv_s, kpe_s = _trunk(
        x_sample.reshape(dec_batch * dec_seq, D_MODEL), PAST_LEN + jnp.arange(dec_seq), p, norms,
        batch=dec_batch, seq=dec_seq, chunk=dec_seq, s0_ret=state_ret, s0_gla=state_gla,
        attn_fn=lambda ql, qp, c, k: _attn_sample(ql, qp, c, k, cache_ckv, cache_kpe, page_table,
                                                  dec_batch, dec_seq),
        act_dtype=F32)

    return (yp.reshape(batch, seq, D_MODEL), ys.reshape(dec_batch, dec_seq, D_MODEL),
            ret_p, gla_p, ckv_p.reshape(batch, seq, KV_LORA), kpe_p.reshape(batch, seq, QK_ROPE),
            ret_s, gla_s, ckv_s.reshape(dec_batch, dec_seq, KV_LORA), kpe_s.reshape(dec_batch, dec_seq, QK_ROPE))
```
